```python
import math
import jax
import jax.numpy as jnp
from jax import lax
import numpy as np

D_MODEL = 2048
BATCH = 2
SEQ = 4096
DEPTH = 2
DEC_BATCH = 128
DEC_SEQ = 1
PAST_LEN = 2048
PAGE_SIZE = 128

HEAD_DIM = 128
N_GROUPS = 4
GROUP_HEADS = D_MODEL // (N_GROUPS * HEAD_DIM)
GROUP_KV = max(1, GROUP_HEADS // 2)
H_FOX = GROUP_HEADS
KV_FOX = GROUP_KV
H_SB = GROUP_HEADS
KV_SB = GROUP_KV
H_GDN = GROUP_HEADS
DK_GDN = HEAD_DIM
DV_GDN = HEAD_DIM
GDN_CONV = 4
GDN_CHUNK = 64
GDN_QKV = H_GDN * (2 * DK_GDN + DV_GDN)
H_DSA = GROUP_HEADS
KV_DSA = GROUP_KV
IDX_HEADS = 16
IDX_DIM = 128
TOPK_MAX = 256
MIX_WIDTH = (H_FOX + H_SB + H_DSA) * HEAD_DIM + H_GDN * DV_GDN
D_FF = 11 * D_MODEL // 4
FFN_CONV = 3
N_BUCKETS = 32
MAX_DISTANCE = 128
Q_BLOCK = 128
FORGET_BIAS_INIT = 3.0
EPS = 1e-6

IN_LAYOUT = (
    ('fox_q', H_FOX * HEAD_DIM), ('fox_k', KV_FOX * HEAD_DIM), ('fox_v', KV_FOX * HEAD_DIM), ('fox_f', H_FOX),
    ('sb_q', H_SB * HEAD_DIM), ('sb_k', KV_SB * HEAD_DIM), ('sb_v', KV_SB * HEAD_DIM),
    ('gdn_qkv', GDN_QKV), ('gdn_a', H_GDN), ('gdn_b', H_GDN), ('gdn_z', H_GDN * DV_GDN),
    ('dsa_q', H_DSA * HEAD_DIM), ('dsa_k', KV_DSA * HEAD_DIM), ('dsa_v', KV_DSA * HEAD_DIM),
    ('dsa_qidx', IDX_HEADS * IDX_DIM), ('dsa_kidx', IDX_DIM), ('dsa_w', IDX_HEADS),
)
D_IN = sum(size for _, size in IN_LAYOUT)

kernel_name = 'hybrid_fox_sb_gdn_dsa_decoder_step'


def split_proj(p):
    out = {}
    off = 0
    for name, size in IN_LAYOUT:
        out[name] = p[..., off:off + size]
        off += size
    return out


def rmsnorm(x, g):
    xf = x.astype(jnp.float32)
    y = xf * lax.rsqrt(jnp.mean(xf * xf, axis=-1, keepdims=True) + EPS)
    return (y * g.astype(jnp.float32)).astype(x.dtype)


def l2norm(x):
    xf = x.astype(jnp.float32)
    return xf * lax.rsqrt(jnp.sum(xf * xf, axis=-1, keepdims=True) + EPS)


def causal_dwconv(u, prev, w):
    width = w.shape[0]
    T = u.shape[1]
    full = jnp.concatenate([prev.astype(u.dtype), u], axis=1)
    out = full[:, 0:T] * w[0]
    for j in range(1, width):
        out = out + full[:, j:j + T] * w[j]
    return out, full[:, T:]


def t5_bucket(rel):
    n = jnp.maximum(rel, 0)
    max_exact = N_BUCKETS // 2
    nf = jnp.maximum(n, 1).astype(jnp.float32)
    large = max_exact + (jnp.log(nf / max_exact) / math.log(MAX_DISTANCE / max_exact)
                         * (N_BUCKETS - max_exact)).astype(jnp.int32)
    large = jnp.minimum(large, N_BUCKETS - 1)
    return jnp.where(n < max_exact, n, large)


def gather_pages(pool, layer, page_table):
    rows = pool[layer, page_table]
    return rows.reshape(rows.shape[0], rows.shape[1] * rows.shape[2], *rows.shape[3:])


def run_query_blocks(fn, qargs):
    T = qargs[0].shape[1]
    if T % Q_BLOCK != 0:
        return fn(*qargs)
    nb = T // Q_BLOCK

    def to_blocks(a):
        return jnp.swapaxes(a.reshape(a.shape[0], nb, Q_BLOCK, *a.shape[2:]), 0, 1)

    out = lax.map(lambda blk: fn(*blk), tuple(to_blocks(a) for a in qargs))
    out = jnp.swapaxes(out, 0, 1)
    return out.reshape(out.shape[0], T, *out.shape[3:])


def fox_attend(q, cq, qpos, k, v, ck):
    B, Tq, H, hd = q.shape
    L, KV = k.shape[1], k.shape[2]
    G = H // KV
    s = jnp.einsum('bqkgd,bskd->bkgqs', q.reshape(B, Tq, KV, G, hd), k).astype(jnp.float32) * hd ** -0.5
    decay = (jnp.moveaxis(cq, 1, 2)[..., :, None] - jnp.moveaxis(ck, 1, 2)[..., None, :]).reshape(B, KV, G, Tq, L)
    mask = (jnp.arange(L, dtype=jnp.int32)[None, None, :] <= qpos[:, :, None])[:, None, None]
    p = jax.nn.softmax(jnp.where(mask, s + decay, -jnp.inf), axis=-1)
    o = jnp.einsum('bkgqs,bskd->bqkgd', p.astype(v.dtype), v)
    return o.reshape(B, Tq, H * hd)


def sb_attend(q, qpos, k, v):
    B, Tq, H, hd = q.shape
    L, KV = k.shape[1], k.shape[2]
    G = H // KV
    z = jnp.einsum('bqkgd,bskd->bkgqs', q.reshape(B, Tq, KV, G, hd), k).astype(jnp.float32) * hd ** -0.5
    mask = (jnp.arange(L, dtype=jnp.int32)[None, None, :] < qpos[:, :, None])[:, None, None]
    log_stay = jnp.where(mask, jax.nn.log_sigmoid(-z), 0.0)
    later = lax.cumsum(log_stay, axis=4, reverse=True) - log_stay
    a = jnp.where(mask, jnp.exp(jax.nn.log_sigmoid(z) + later), 0.0)
    o = jnp.einsum('bkgqs,bskd->bqkgd', a.astype(v.dtype), v)
    return o.reshape(B, Tq, H * hd)


def dsa_attend(q, qidx, widx, qpos, k, v, kidx, rel_bias, topk):
    B, Tq, H, hd = q.shape
    L, KV = k.shape[1], k.shape[2]
    G = H // KV
    si = jnp.einsum('bqhd,bsd->bqhs', qidx, kidx).astype(jnp.float32) * IDX_DIM ** -0.5
    score = jnp.einsum('bqhs,bqh->bqs', jax.nn.relu(si), widx.astype(jnp.float32)) * IDX_HEADS ** -0.5
    kpos = jnp.arange(L, dtype=jnp.int32)
    score = jnp.where(kpos[None, None, :] <= qpos[:, :, None], score, -jnp.inf)
    _, idx = lax.top_k(score, topk)
    valid = idx <= qpos[:, :, None]
    take = jax.vmap(lambda rows, ii: rows[ii])
    k_sel = take(k, idx)
    v_sel = take(v, idx)
    s = jnp.einsum('bqkgd,bqnkd->bkgqn', q.reshape(B, Tq, KV, G, hd), k_sel).astype(jnp.float32) * hd ** -0.5
    bias = rel_bias[t5_bucket(qpos[:, :, None] - idx)].astype(jnp.float32)
    bias = jnp.transpose(bias, (0, 3, 1, 2)).reshape(B, KV, G, Tq, topk)
    p = jax.nn.softmax(jnp.where(valid[:, None, None], s + bias, -jnp.inf), axis=-1)
    o = jnp.einsum('bkgqn,bqnkd->bqkgd', p.astype(v.dtype), v_sel)
    return o.reshape(B, Tq, H * hd)


def gdn_chunked(q, k, v, g, beta, s0):
    B, T, H, dk = q.shape
    dv = v.shape[-1]
    C = GDN_CHUNK
    N = T // C

    def chunks(a):
        a = jnp.moveaxis(a, 2, 1)
        return a.reshape(B, H, N, C, *a.shape[3:])

    q, k, v, g, beta = (chunks(a) for a in (q, k, v, g, beta))
    gc = jnp.cumsum(g, axis=-1)
    lower = jnp.tril(jnp.ones((C, C), dtype=bool))
    strict = jnp.tril(jnp.ones((C, C), dtype=bool), -1)
    decay = jnp.exp(jnp.where(lower, gc[..., :, None] - gc[..., None, :], -jnp.inf))
    kb = k * beta[..., None]
    lmat = jnp.where(strict, jnp.einsum('bhnid,bhnjd->bhnij', kb, k) * decay, 0.0)
    eye = jnp.eye(C, dtype=jnp.float32)
    tinv = lax.linalg.triangular_solve(lmat + eye, jnp.broadcast_to(eye, lmat.shape),
                                       left_side=True, lower=True, unit_diagonal=True)
    u = jnp.einsum('bhnij,bhnje->bhnie', tinv, v * beta[..., None])
    w = jnp.einsum('bhnij,bhnjd->bhnid', tinv, kb * jnp.exp(gc)[..., None])
    aqk = jnp.where(lower, jnp.einsum('bhnid,bhnjd->bhnij', q, k) * decay, 0.0)
    q_dec = q * jnp.exp(gc)[..., None]
    g_last = gc[..., -1]
    k_dec = k * jnp.exp(g_last[..., None] - gc)[..., None]

    def step(s, xs):
        u_c, w_c, qd_c, a_c, kd_c, gl_c = xs
        v_new = u_c - jnp.einsum('bhcd,bhde->bhce', w_c, s)
        o = jnp.einsum('bhcd,bhde->bhce', qd_c, s) + jnp.einsum('bhij,bhje->bhie', a_c, v_new)
        s = s * jnp.exp(gl_c)[..., None, None] + jnp.einsum('bhcd,bhce->bhde', kd_c, v_new)
        return s, o

    xs = tuple(jnp.moveaxis(a, 2, 0) for a in (u, w, q_dec, aqk, k_dec, g_last))
    s_fin, o = lax.scan(step, s0, xs)
    o = jnp.moveaxis(o, 0, 2).reshape(B, H, T, dv)
    return jnp.moveaxis(o, 1, 2), s_fin


def gdn_recurrent(q, k, v, g, beta, s0):
    def step(s, xs):
        qt, kt, vt, gt, bt = xs
        s = s * jnp.exp(gt)[..., None, None]
        v_old = jnp.einsum('bhkv,bhk->bhv', s, kt)
        s = s + jnp.einsum('bhk,bhv->bhkv', kt, (vt - v_old) * bt[..., None])
        return s, jnp.einsum('bhkv,bhk->bhv', s, qt)

    s_fin, o = lax.scan(step, s0, tuple(jnp.moveaxis(a, 1, 0) for a in (q, k, v, g, beta)))
    return jnp.moveaxis(o, 0, 1), s_fin


def token_mixers(h, w_in, b_fox_f, gdn_a_log, gdn_dt_bias, gdn_conv_w, gdn_norm_w, w_out, rel_bias, past):
    B, T, _ = h.shape
    f32 = jnp.float32
    P = 0 if past is None else past['fox_k'].shape[1]
    qpos = jnp.broadcast_to(P + jnp.arange(T, dtype=jnp.int32), (B, T))
    pc = split_proj(jnp.einsum('btd,de->bte', h, w_in))

    def extend(name, new):
        if past is None:
            return new
        return jnp.concatenate([past[name].astype(new.dtype), new], axis=1)

    fq = pc['fox_q'].reshape(B, T, H_FOX, HEAD_DIM)
    fk = pc['fox_k'].reshape(B, T, KV_FOX, HEAD_DIM)
    fv = pc['fox_v'].reshape(B, T, KV_FOX, HEAD_DIM)
    flogf = jax.nn.log_sigmoid(pc['fox_f'].astype(f32) + b_fox_f.astype(f32))
    fk_all = extend('fox_k', fk)
    fv_all = extend('fox_v', fv)
    fcum = jnp.cumsum(extend('fox_logf', flogf), axis=1)
    o_fox = run_query_blocks(lambda q, cq, qp: fox_attend(q, cq, qp, fk_all, fv_all, fcum),
                             (fq, fcum[:, P:], qpos))

    sq = pc['sb_q'].reshape(B, T, H_SB, HEAD_DIM)
    sk = pc['sb_k'].reshape(B, T, KV_SB, HEAD_DIM)
    sv = pc['sb_v'].reshape(B, T, KV_SB, HEAD_DIM)
    sk_all = extend('sb_k', sk)
    sv_all = extend('sb_v', sv)
    o_sb = run_query_blocks(lambda q, qp: sb_attend(q, qp, sk_all, sv_all), (sq, qpos))

    u = pc['gdn_qkv']
    prev = jnp.zeros((B, GDN_CONV - 1, GDN_QKV), u.dtype) if past is None else past['gdn_conv']
    uc, conv_gdn = causal_dwconv(u, prev, gdn_conv_w)
    uc = jax.nn.silu(uc)
    nq = H_GDN * DK_GDN
    gq = l2norm(uc[..., :nq].reshape(B, T, H_GDN, DK_GDN)) * DK_GDN ** -0.5
    gk = l2norm(uc[..., nq:2 * nq].reshape(B, T, H_GDN, DK_GDN))
    gv = uc[..., 2 * nq:].reshape(B, T, H_GDN, DV_GDN).astype(f32)
    glog = -jnp.exp(gdn_a_log.astype(f32)) * jax.nn.softplus(pc['gdn_a'].astype(f32) + gdn_dt_bias.astype(f32))
    gbeta = jax.nn.sigmoid(pc['gdn_b'].astype(f32))
    if past is None:
        go, s_gdn = gdn_chunked(gq, gk, gv, glog, gbeta, jnp.zeros((B, H_GDN, DK_GDN, DV_GDN), f32))
    else:
        go, s_gdn = gdn_recurrent(gq, gk, gv, glog, gbeta, past['gdn_s'].astype(f32))
    gz = pc['gdn_z'].reshape(B, T, H_GDN, DV_GDN).astype(f32)
    go = go * lax.rsqrt(jnp.mean(go * go, axis=-1, keepdims=True) + EPS) * gdn_norm_w.astype(f32) * jax.nn.silu(gz)
    o_gdn = go.reshape(B, T, H_GDN * DV_GDN).astype(h.dtype)

    dq = pc['dsa_q'].reshape(B, T, H_DSA, HEAD_DIM)
    dk = pc['dsa_k'].reshape(B, T, KV_DSA, HEAD_DIM)
    dv = pc['dsa_v'].reshape(B, T, KV_DSA, HEAD_DIM)
    dqidx = pc['dsa_qidx'].reshape(B, T, IDX_HEADS, IDX_DIM)
    dkidx = pc['dsa_kidx']
    dw = pc['dsa_w']
    dk_all = extend('dsa_k', dk)
    dv_all = extend('dsa_v', dv)
    dkidx_all = extend('dsa_kidx', dkidx)
    topk = min(TOPK_MAX, dk_all.shape[1] // 4)
    o_dsa = run_query_blocks(
        lambda q, qi, wi, qp: dsa_attend(q, qi, wi, qp, dk_all, dv_all, dkidx_all, rel_bias, topk),
        (dq, dqidx, dw, qpos))

    mixed = jnp.concatenate([o_fox, o_sb, o_gdn, o_dsa], axis=-1)
    out = jnp.einsum('btm,md->btd', mixed, w_out)
    states = (fk, fv, flogf, sk, sv, dk, dv, dkidx, s_gdn, conv_gdn)
    return out, states


def conv_ffn(h, w_up, conv_w, conv_b, w_down, prev):
    up = jnp.einsum('btd,df->btf', h, w_up)
    if prev is None:
        prev = jnp.zeros((up.shape[0], FFN_CONV - 1, up.shape[-1]), up.dtype)
    uc, new_prev = causal_dwconv(up, prev, conv_w)
    gate, val = jnp.split(uc + conv_b, 2, axis=-1)
    return jnp.einsum('btf,fd->btd', jax.nn.silu(gate) * val, w_down), new_prev


def trunk_layer(x, c, lw, rel_bias, past):
    (w_mod, b_mod, g_mix_pre, g_mix_post, g_ffn_pre, g_ffn_post, w_in, b_fox_f, gdn_a_log, gdn_dt_bias,
     gdn_conv_w, gdn_norm_w, w_out, w_up, ffn_conv_w, ffn_conv_b, w_down) = lw
    mod = jnp.einsum('bd,de->be', jax.nn.silu(c), w_mod) + b_mod
    sh1, sc1, g1, sh2, sc2, g2 = jnp.split(mod[:, None, :], 6, axis=-1)
    hm = rmsnorm(x, g_mix_pre) * (1 + sc1) + sh1
    m, mstates = token_mixers(hm, w_in, b_fox_f, gdn_a_log, gdn_dt_bias, gdn_conv_w, gdn_norm_w, w_out,
                              rel_bias, past)
    x = x + g1 * rmsnorm(m, g_mix_post)
    hf = rmsnorm(x, g_ffn_pre) * (1 + sc2) + sh2
    f, fstate = conv_ffn(hf, w_up, ffn_conv_w, ffn_conv_b, w_down, None if past is None else past['ffn_conv'])
    x = x + g2 * rmsnorm(f, g_ffn_post)
    return x, mstates + (fstate,)


def setup_inputs(seed: int = 0) -> dict:
    key = jax.random.key(seed)
    kk = jax.random.split(key, 40)
    f32 = jnp.float32

    def nrm(k, shape, scale=1.0):
        return jax.random.normal(k, shape, f32) * scale

    n_pages = PAST_LEN // PAGE_SIZE
    n_used = DEC_BATCH * n_pages
    n_pool = (5 * n_used + 3) // 4
    page_table = jax.random.permutation(kk[0], n_pool)[:n_used].reshape(DEC_BATCH, n_pages).astype(jnp.int32)
    pool = (DEPTH, n_pool, PAGE_SIZE)
    dt = jnp.exp(jax.random.uniform(kk[1], (DEPTH, H_GDN), f32, math.log(1e-3), math.log(1e-1)))
    return {
        'x_prompt': nrm(kk[2], (BATCH, SEQ, D_MODEL)),
        'x_sample': nrm(kk[3], (DEC_BATCH, DEC_SEQ, D_MODEL)),
        'cache_fox_k': nrm(kk[4], pool + (KV_FOX, HEAD_DIM)),
        'cache_fox_v': nrm(kk[5], pool + (KV_FOX, HEAD_DIM)),
        'cache_fox_logf': jax.nn.log_sigmoid(FORGET_BIAS_INIT + nrm(kk[6], pool + (H_FOX,))),
        'cache_sb_k': nrm(kk[7], pool + (KV_SB, HEAD_DIM)),
        'cache_sb_v': nrm(kk[8], pool + (KV_SB, HEAD_DIM)),
        'cache_dsa_k': nrm(kk[9], pool + (KV_DSA, HEAD_DIM)),
        'cache_dsa_v': nrm(kk[10], pool + (KV_DSA, HEAD_DIM)),
        'cache_dsa_kidx': nrm(kk[11], pool + (IDX_DIM,)),
        'state_gdn_s': nrm(kk[12], (DEPTH, DEC_BATCH, H_GDN, DK_GDN, DV_GDN), 0.1),
        'state_gdn_conv': nrm(kk[13], (DEPTH, DEC_BATCH, GDN_CONV - 1, GDN_QKV)),
        'state_ffn_conv': nrm(kk[14], (DEPTH, DEC_BATCH, FFN_CONV - 1, 2 * D_FF)),
        'page_table': page_table,
        'c_prompt': nrm(kk[15], (BATCH, D_MODEL)),
        'c_sample': nrm(kk[16], (DEC_BATCH, D_MODEL)),
        'w_mod': nrm(kk[17], (DEPTH, D_MODEL, 6 * D_MODEL), 0.5 * D_MODEL ** -0.5),
        'b_mod': nrm(kk[18], (DEPTH, 6 * D_MODEL), 0.02),
        'g_mix_pre': 1.0 + nrm(kk[19], (DEPTH, D_MODEL), 0.05),
        'g_mix_post': 1.0 + nrm(kk[20], (DEPTH, D_MODEL), 0.05),
        'g_ffn_pre': 1.0 + nrm(kk[21], (DEPTH, D_MODEL), 0.05),
        'g_ffn_post': 1.0 + nrm(kk[22], (DEPTH, D_MODEL), 0.05),
        'w_in': nrm(kk[23], (DEPTH, D_MODEL, D_IN), D_MODEL ** -0.5),
        'b_fox_f': FORGET_BIAS_INIT + nrm(kk[24], (DEPTH, H_FOX), 0.5),
        'gdn_a_log': jnp.log(jax.random.uniform(kk[25], (DEPTH, H_GDN), f32, 1.0, 16.0)),
        'gdn_dt_bias': dt + jnp.log(-jnp.expm1(-dt)),
        'gdn_conv_w': nrm(kk[26], (DEPTH, GDN_CONV, GDN_QKV), GDN_CONV ** -0.5),
        'gdn_norm_w': 1.0 + nrm(kk[27], (DEPTH, DV_GDN), 0.05),
        'rel_bias': nrm(kk[28], (N_BUCKETS, H_DSA), 0.5),
        'w_out': nrm(kk[29], (DEPTH, MIX_WIDTH, D_MODEL), MIX_WIDTH ** -0.5),
        'w_up': nrm(kk[30], (DEPTH, D_MODEL, 2 * D_FF), D_MODEL ** -0.5),
        'ffn_conv_w': nrm(kk[31], (DEPTH, FFN_CONV, 2 * D_FF), FFN_CONV ** -0.5),
        'ffn_conv_b': nrm(kk[32], (DEPTH, 2 * D_FF), 0.02),
        'w_down': nrm(kk[33], (DEPTH, D_FF, D_MODEL), D_FF ** -0.5),
    }


def reference(x_prompt, x_sample, cache_fox_k, cache_fox_v, cache_fox_logf, cache_sb_k, cache_sb_v,
              cache_dsa_k, cache_dsa_v, cache_dsa_kidx, state_gdn_s, state_gdn_conv, state_ffn_conv,
              page_table, c_prompt, c_sample, w_mod, b_mod, g_mix_pre, g_mix_post, g_ffn_pre, g_ffn_post,
              w_in, b_fox_f, gdn_a_log, gdn_dt_bias, gdn_conv_w, gdn_norm_w, rel_bias, w_out, w_up,
              ffn_conv_w, ffn_conv_b, w_down):
    layer_weights = (w_mod, b_mod, g_mix_pre, g_mix_post, g_ffn_pre, g_ffn_post, w_in, b_fox_f, gdn_a_log,
                     gdn_dt_bias, gdn_conv_w, gdn_norm_w, w_out, w_up, ffn_conv_w, ffn_conv_b, w_down)
    xp = x_prompt
    xs = x_sample
    st_prompt = []
    st_sample = []
    for l in range(DEPTH):
        lw = tuple(a[l] for a in layer_weights)
        past = {
            'fox_k': gather_pages(cache_fox_k, l, page_table),
            'fox_v': gather_pages(cache_fox_v, l, page_table),
            'fox_logf': gather_pages(cache_fox_logf, l, page_table),
            'sb_k': gather_pages(cache_sb_k, l, page_table),
            'sb_v': gather_pages(cache_sb_v, l, page_table),
            'dsa_k': gather_pages(cache_dsa_k, l, page_table),
            'dsa_v': gather_pages(cache_dsa_v, l, page_table),
            'dsa_kidx': gather_pages(cache_dsa_kidx, l, page_table),
            'gdn_s': state_gdn_s[l],
            'gdn_conv': state_gdn_conv[l],
            'ffn_conv': state_ffn_conv[l],
        }
        xp, sp = trunk_layer(xp, c_prompt, lw, rel_bias, None)
        xs, ss = trunk_layer(xs, c_sample, lw, rel_bias, past)
        st_prompt.append(sp)
        st_sample.append(ss)
    (fox_k_p, fox_v_p, fox_logf_p, sb_k_p, sb_v_p, dsa_k_p, dsa_v_p, dsa_kidx_p, gdn_s_p, gdn_conv_p,
     ffn_conv_p) = [jnp.stack(z) for z in zip(*st_prompt)]
    (fox_k_s, fox_v_s, fox_logf_s, sb_k_s, sb_v_s, dsa_k_s, dsa_v_s, dsa_kidx_s, gdn_s_s, gdn_conv_s,
     ffn_conv_s) = [jnp.stack(z) for z in zip(*st_sample)]
    return (xp, xs, fox_k_p, fox_k_s, fox_v_p, fox_v_s, fox_logf_p, fox_logf_s, sb_k_p, sb_k_s, sb_v_p, sb_v_s,
            dsa_k_p, dsa_k_s, dsa_v_p, dsa_v_s, dsa_kidx_p, dsa_kidx_s, gdn_s_p, gdn_s_s, gdn_conv_p, gdn_conv_s,
            ffn_conv_p, ffn_conv_s)
```

```python
import functools
import math

import numpy as np
import jax
import jax.numpy as jnp
from jax import lax
from jax.experimental import pallas as pl
from jax.experimental.pallas import tpu as pltpu

F32 = jnp.float32
BF16 = jnp.bfloat16

HEAD_DIM = 128
N_HEADS = 4
N_KV = 2
IDX_HEADS = 16
IDX_DIM = 128
GDN_CONV = 4
GDN_CHUNK = 64
FFN_CONV = 3
TOPK_MAX = 256
N_BUCKETS = 32
MAX_DISTANCE = 128
EPS = 1e-6
NEG_INF = float("-inf")

LANES = 128
SUBLANES = 8
VMEM_LIMIT = 56 * 1024 * 1024

C_FOX_Q, C_FOX_K, C_FOX_V = 0, 512, 768
C_SB_Q, C_SB_K, C_SB_V = 1024, 1536, 1792
C_GDN_QKV, C_GDN_Z = 2048, 3584
C_DSA_QIDX = 4096
C_DSA_Q, C_DSA_K, C_DSA_V = 6144, 6656, 6912
C_DSA_KIDX, C_SMALL = 7168, 7296
D_IN_PAD = 7680
S_FOX_F, S_GDN_A, S_GDN_B, S_DSA_W = 0, 4, 8, 12


def _in_proj_permutation(d_model):
    gdn_qkv = N_HEADS * 3 * HEAD_DIM
    layout = (
        ('fox_q', 512), ('fox_k', 256), ('fox_v', 256), ('fox_f', 4),
        ('sb_q', 512), ('sb_k', 256), ('sb_v', 256),
        ('gdn_qkv', gdn_qkv), ('gdn_a', 4), ('gdn_b', 4), ('gdn_z', 512),
        ('dsa_q', 512), ('dsa_k', 256), ('dsa_v', 256),
        ('dsa_qidx', IDX_HEADS * IDX_DIM), ('dsa_kidx', IDX_DIM), ('dsa_w', IDX_HEADS),
    )
    off = {}
    o = 0
    for name, size in layout:
        off[name] = (o, size)
        o += size
    order = ('fox_q', 'fox_k', 'fox_v', 'sb_q', 'sb_k', 'sb_v', 'gdn_qkv', 'gdn_z', 'dsa_qidx', 'dsa_q', 'dsa_k',
             'dsa_v', 'dsa_kidx', 'fox_f', 'gdn_a', 'gdn_b', 'dsa_w')
    idx = np.concatenate([np.arange(off[n][0], off[n][0] + off[n][1]) for n in order])
    return idx, o


def _cparams(*sem):
    return pltpu.CompilerParams(dimension_semantics=sem, vmem_limit_bytes=VMEM_LIMIT)


def _rms(x, g):
    return x * lax.rsqrt(jnp.mean(x * x, axis=-1, keepdims=True) + EPS) * g


def _silu(x):
    return x * (1.0 / (1.0 + jnp.exp(-x)))


def _log_sigmoid(x):
    return jnp.minimum(x, 0.0) - jnp.log(1.0 + jnp.exp(-jnp.abs(x)))


def _split3(x):
    hi = x.astype(BF16)
    r1 = x - hi.astype(F32)
    mid = r1.astype(BF16)
    lo = (r1 - mid.astype(F32)).astype(BF16)
    return hi, mid, lo


def _dot(a, b):
    return jnp.dot(a, b, preferred_element_type=F32)


def _dot_nt(a, b):
    return lax.dot_general(a, b, (((1,), (1,)), ((), ())), preferred_element_type=F32)


def _dot_tn(a, b):
    return lax.dot_general(a, b, (((0,), (0,)), ((), ())), preferred_element_type=F32)


def _mod_kernel(c_ref, w_ref, b_ref, o_ref):
    a = _silu(c_ref[...]).astype(BF16)
    o_ref[...] = _dot(a, w_ref[...].astype(BF16)) + b_ref[...]


def _modulation(c, w_mod, b_mod):
    m, d = c.shape
    n = w_mod.shape[1]
    tn = 1024
    return pl.pallas_call(
        _mod_kernel,
        grid=(n // tn,),
        in_specs=[pl.BlockSpec((m, d), lambda j: (0, 0)),
                  pl.BlockSpec((d, tn), lambda j: (0, j)),
                  pl.BlockSpec((1, tn), lambda j: (0, j))],
        out_specs=pl.BlockSpec((m, tn), lambda j: (0, j)),
        out_shape=jax.ShapeDtypeStruct((m, n), F32),
        compiler_params=_cparams("parallel"),
        name="modulation",
    )(c, w_mod, b_mod.reshape(1, n))


def _mod_spec(mod, chunk, tm, rows_per_seq, d):
    if rows_per_seq == 1:
        return mod, pl.BlockSpec((tm, d), lambda i, *_: (i, chunk))
    tiles_per_seq = rows_per_seq // tm
    return (mod.reshape(mod.shape[0], 1, mod.shape[1]),
            pl.BlockSpec((None, 1, d), lambda i, *_: (i // tiles_per_seq, 0, chunk)))


def _in_proj_kernel(x_ref, g_ref, sc_ref, sh_ref, w_ref, o_ref, h_ref):
    @pl.when(pl.program_id(1) == 0)
    def _():
        h = _rms(x_ref[...], g_ref[...]) * (1.0 + sc_ref[...]) + sh_ref[...]
        h_ref[...] = h.astype(BF16)

    o_ref[...] = _dot(h_ref[...], w_ref[...])


def _in_proj(x, g, mod, w, rows_per_seq, tm, tn):
    m, d = x.shape
    n = w.shape[1]
    sc, sc_spec = _mod_spec(mod, 1, tm, rows_per_seq, d)
    sh, sh_spec = _mod_spec(mod, 0, tm, rows_per_seq, d)
    return pl.pallas_call(
        _in_proj_kernel,
        grid=(m // tm, n // tn),
        in_specs=[pl.BlockSpec((tm, d), lambda i, j: (i, 0)),
                  pl.BlockSpec((1, d), lambda i, j: (0, 0)),
                  sc_spec, sh_spec,
                  pl.BlockSpec((d, tn), lambda i, j: (0, j))],
        out_specs=pl.BlockSpec((tm, tn), lambda i, j: (i, j)),
        out_shape=jax.ShapeDtypeStruct((m, n), F32),
        scratch_shapes=[pltpu.VMEM((tm, d), BF16)],
        compiler_params=_cparams("parallel", "arbitrary"),
        name="in_proj",
    )(x, g.reshape(1, d), sc, sh, w)


def _out_proj_kernel(a0_ref, a1_ref, a2_ref, a3_ref, w_ref, x_ref, g_ref, gate_ref, o_ref):
    m = None
    for n, a_ref in enumerate((a0_ref, a1_ref, a2_ref, a3_ref)):
        k = a_ref.shape[1]
        part = _dot(a_ref[...].astype(BF16), w_ref[n * k:(n + 1) * k, :])
        m = part if m is None else m + part
    o_ref[...] = x_ref[...] + gate_ref[...] * _rms(m, g_ref[...])


def _out_proj(mixed, w, x, g, mod, rows_per_seq, tm):
    m, k4 = mixed[0].shape
    k, d = w.shape
    gate, gate_spec = _mod_spec(mod, 2, tm, rows_per_seq, d)
    a_spec = pl.BlockSpec((tm, k4), lambda i: (i, 0))
    return pl.pallas_call(
        _out_proj_kernel,
        grid=(m // tm,),
        in_specs=[a_spec, a_spec, a_spec, a_spec,
                  pl.BlockSpec((k, d), lambda i: (0, 0)),
                  pl.BlockSpec((tm, d), lambda i: (i, 0)),
                  pl.BlockSpec((1, d), lambda i: (0, 0)),
                  gate_spec],
        out_specs=pl.BlockSpec((tm, d), lambda i: (i, 0)),
        out_shape=jax.ShapeDtypeStruct((m, d), F32),
        compiler_params=_cparams("parallel"),
        name="out_proj",
    )(*mixed, w, x, g.reshape(1, d), gate)


def _shift_rows(u, prev, k):
    rolled = pltpu.roll(u, k, axis=0)
    row = lax.broadcasted_iota(jnp.int32, u.shape, 0)
    out = rolled
    for r in range(k):
        out = jnp.where(row == r, prev[SUBLANES - k + r:SUBLANES - k + r + 1, :], out)
    return out


def _ffn_prompt_kernel(x_ref, g_ref, sc_ref, sh_ref, wg_ref, wv_ref, cwg_ref, cwv_ref, cbg_ref, cbv_ref,
                       wd_ref, gpost_ref, gate_ref, o_ref, sg_ref, sv_ref,
                       h_ref, acc_ref, cg_ref, cv_ref, *, tiles_per_seq):
    i = pl.program_id(0)
    j = pl.program_id(1)

    @pl.when(j == 0)
    def _():
        h = _rms(x_ref[...], g_ref[...]) * (1.0 + sc_ref[...]) + sh_ref[...]
        h_ref[...] = h.astype(BF16)
        acc_ref[...] = jnp.zeros_like(acc_ref)

    keep = (i % tiles_per_seq != 0).astype(F32)

    def conv(u, carry_ref, cw_ref, cb_ref):
        prev = carry_ref[j] * keep
        out = cw_ref[0:1, :] * _shift_rows(u, prev, 2) + cw_ref[1:2, :] * _shift_rows(u, prev, 1)
        out = out + cw_ref[2:3, :] * u + cb_ref[...]
        carry_ref[j] = u[u.shape[0] - SUBLANES:, :]
        return out

    ug = _dot(h_ref[...], wg_ref[...])
    uv = _dot(h_ref[...], wv_ref[...])
    sg_ref[...] = ug[ug.shape[0] - SUBLANES:, :]
    sv_ref[...] = uv[uv.shape[0] - SUBLANES:, :]
    gate = conv(ug, cg_ref, cwg_ref, cbg_ref)
    val = conv(uv, cv_ref, cwv_ref, cbv_ref)
    hid = (_silu(gate) * val).astype(BF16)
    acc_ref[...] += _dot(hid, wd_ref[...])

    @pl.when(j == pl.num_programs(1) - 1)
    def _():
        o_ref[...] = x_ref[...] + gate_ref[...] * _rms(acc_ref[...], gpost_ref[...])


def _ffn_prompt(x, g_pre, mod, w_up, conv_w, conv_b, w_down, g_post, rows_per_seq, tm, tf):
    m, d = x.shape
    f = w_down.shape[0]
    nf = f // tf
    nseq = m // rows_per_seq
    tiles_per_seq = rows_per_seq // tm
    sc, sc_spec = _mod_spec(mod, 4, tm, rows_per_seq, d)
    sh, sh_spec = _mod_spec(mod, 3, tm, rows_per_seq, d)
    gate, gate_spec = _mod_spec(mod, 5, tm, rows_per_seq, d)
    state_spec = pl.BlockSpec((None, SUBLANES, tf), lambda i, j: (i // tiles_per_seq, 0, j))
    out, sg, sv = pl.pallas_call(
        functools.partial(_ffn_prompt_kernel, tiles_per_seq=tiles_per_seq),
        grid=(m // tm, nf),
        in_specs=[pl.BlockSpec((tm, d), lambda i, j: (i, 0)),
                  pl.BlockSpec((1, d), lambda i, j: (0, 0)),
                  sc_spec, sh_spec,
                  pl.BlockSpec((d, tf), lambda i, j: (0, j)),
                  pl.BlockSpec((d, tf), lambda i, j: (0, j + nf)),
                  pl.BlockSpec((FFN_CONV, tf), lambda i, j: (0, j)),
                  pl.BlockSpec((FFN_CONV, tf), lambda i, j: (0, j + nf)),
                  pl.BlockSpec((1, tf), lambda i, j: (0, j)),
                  pl.BlockSpec((1, tf), lambda i, j: (0, j + nf)),
                  pl.BlockSpec((tf, d), lambda i, j: (j, 0)),
                  pl.BlockSpec((1, d), lambda i, j: (0, 0)),
                  gate_spec],
        out_specs=[pl.BlockSpec((tm, d), lambda i, j: (i, 0)), state_spec, state_spec],
        out_shape=[jax.ShapeDtypeStruct((m, d), F32),
                   jax.ShapeDtypeStruct((nseq, SUBLANES, f), F32),
                   jax.ShapeDtypeStruct((nseq, SUBLANES, f), F32)],
        scratch_shapes=[pltpu.VMEM((tm, d), BF16), pltpu.VMEM((tm, d), F32),
                        pltpu.VMEM((nf, SUBLANES, tf), F32), pltpu.VMEM((nf, SUBLANES, tf), F32)],
        compiler_params=_cparams("arbitrary", "arbitrary"),
        name="ffn_prompt",
    )(x, g_pre.reshape(1, d), sc, sh, w_up, w_up, conv_w, conv_w, conv_b.reshape(1, 2 * f),
      conv_b.reshape(1, 2 * f), w_down, g_post.reshape(1, d), gate)
    state = jnp.concatenate([sg, sv], axis=-1)[:, SUBLANES - (FFN_CONV - 1):, :]
    return out, state


def _ffn_sample_kernel(x_ref, g_ref, sc_ref, sh_ref, wg_ref, wv_ref, cwg_ref, cwv_ref, cbg_ref, cbv_ref,
                       p0g_ref, p0v_ref, p1g_ref, p1v_ref, wd_ref, gpost_ref, gate_ref,
                       o_ref, ug_ref, uv_ref, h_ref, acc_ref):
    j = pl.program_id(1)

    @pl.when(j == 0)
    def _():
        h = _rms(x_ref[...], g_ref[...]) * (1.0 + sc_ref[...]) + sh_ref[...]
        h_ref[...] = h.astype(BF16)
        acc_ref[...] = jnp.zeros_like(acc_ref)

    ug = _dot(h_ref[...], wg_ref[...])
    uv = _dot(h_ref[...], wv_ref[...])
    ug_ref[...] = ug
    uv_ref[...] = uv
    gate = cwg_ref[0:1, :] * p0g_ref[...] + cwg_ref[1:2, :] * p1g_ref[...] + cwg_ref[2:3, :] * ug + cbg_ref[...]
    val = cwv_ref[0:1, :] * p0v_ref[...] + cwv_ref[1:2, :] * p1v_ref[...] + cwv_ref[2:3, :] * uv + cbv_ref[...]
    hid = (_silu(gate) * val).astype(BF16)
    acc_ref[...] += _dot(hid, wd_ref[...])

    @pl.when(j == pl.num_programs(1) - 1)
    def _():
        o_ref[...] = x_ref[...] + gate_ref[...] * _rms(acc_ref[...], gpost_ref[...])


def _ffn_sample(x, g_pre, mod, w_up, conv_w, conv_b, w_down, g_post, prev, tf):
    m, d = x.shape
    f = w_down.shape[0]
    nf = f // tf
    tm = m
    sc, sc_spec = _mod_spec(mod, 4, tm, 1, d)
    sh, sh_spec = _mod_spec(mod, 3, tm, 1, d)
    gate, gate_spec = _mod_spec(mod, 5, tm, 1, d)
    prev2 = prev.reshape(m, (FFN_CONV - 1) * 2 * f)
    up_spec = pl.BlockSpec((tm, tf), lambda i, j: (i, j))
    out, up_g, up_v = pl.pallas_call(
        _ffn_sample_kernel,
        grid=(1, nf),
        in_specs=[pl.BlockSpec((tm, d), lambda i, j: (i, 0)),
                  pl.BlockSpec((1, d), lambda i, j: (0, 0)),
                  sc_spec, sh_spec,
                  pl.BlockSpec((d, tf), lambda i, j: (0, j)),
                  pl.BlockSpec((d, tf), lambda i, j: (0, j + nf)),
                  pl.BlockSpec((FFN_CONV, tf), lambda i, j: (0, j)),
                  pl.BlockSpec((FFN_CONV, tf), lambda i, j: (0, j + nf)),
                  pl.BlockSpec((1, tf), lambda i, j: (0, j)),
                  pl.BlockSpec((1, tf), lambda i, j: (0, j + nf)),
                  pl.BlockSpec((tm, tf), lambda i, j: (i, j)),
                  pl.BlockSpec((tm, tf), lambda i, j: (i, j + nf)),
                  pl.BlockSpec((tm, tf), lambda i, j: (i, j + 2 * nf)),
                  pl.BlockSpec((tm, tf), lambda i, j: (i, j + 3 * nf)),
                  pl.BlockSpec((tf, d), lambda i, j: (j, 0)),
                  pl.BlockSpec((1, d), lambda i, j: (0, 0)),
                  gate_spec],
        out_specs=[pl.BlockSpec((tm, d), lambda i, j: (i, 0)), up_spec, up_spec],
        out_shape=[jax.ShapeDtypeStruct((m, d), F32), jax.ShapeDtypeStruct((m, f), F32),
                   jax.ShapeDtypeStruct((m, f), F32)],
        scratch_shapes=[pltpu.VMEM((tm, d), BF16), pltpu.VMEM((tm, d), F32)],
        compiler_params=_cparams("arbitrary", "arbitrary"),
        name="ffn_sample",
    )(x, g_pre.reshape(1, d), sc, sh, w_up, w_up, conv_w, conv_w, conv_b.reshape(1, 2 * f),
      conv_b.reshape(1, 2 * f), prev2, prev2, prev2, prev2, w_down, g_post.reshape(1, d), gate)
    return out, jnp.concatenate([up_g, up_v], axis=-1)


def _tri(n, fn):
    r = lax.broadcasted_iota(jnp.int32, (n, n), 0)
    c = lax.broadcasted_iota(jnp.int32, (n, n), 1)
    return jnp.where(fn(r, c), 1.0, 0.0).astype(BF16)


def _dot_exact01(m01, x):
    hi, mid, lo = _split3(x)
    return _dot(m01, hi) + _dot(m01, mid) + _dot(m01, lo)


def _fox_gate_kernel(s_ref, b_ref, lf_ref, cum_ref, carry_ref):
    @pl.when(pl.program_id(1) == 0)
    def _():
        carry_ref[...] = jnp.zeros_like(carry_ref)

    lf = _log_sigmoid(s_ref[...] + b_ref[...])
    n = lf.shape[0]
    cum = _dot_exact01(_tri(n, lambda r, c: r >= c), lf) + carry_ref[0:1, :]
    lf_ref[...] = lf
    cum_ref[...] = cum
    carry_ref[...] = jnp.broadcast_to(cum[n - 1:n, :], carry_ref.shape)


def _fox_gates(pc, b_row, nseq, t, tc=256):
    nt = t // tc
    cb = C_SMALL // LANES
    spec = pl.BlockSpec((tc, LANES), lambda b, i: (b * nt + i, 0))
    return pl.pallas_call(
        _fox_gate_kernel,
        grid=(nseq, nt),
        in_specs=[pl.BlockSpec((tc, LANES), lambda b, i: (b * nt + i, cb)),
                  pl.BlockSpec((1, LANES), lambda b, i: (0, 0))],
        out_specs=[spec, spec],
        out_shape=[jax.ShapeDtypeStruct((nseq * t, LANES), F32)] * 2,
        scratch_shapes=[pltpu.VMEM((SUBLANES, LANES), F32)],
        compiler_params=_cparams("parallel", "arbitrary"),
        name="fox_gates",
    )(pc, b_row)


MASKED = -1e30


def _flash_kernel(*refs, mode, tq, tk):
    if mode == 'fox':
        q_ref, k_ref, v_ref, cq_ref, ck_ref, o_ref, m_ref, l_ref, acc_ref = refs
    else:
        q_ref, k_ref, v_ref, bias_ref, sc_ref, thr_ref, o_ref, m_ref, l_ref, acc_ref = refs
    qi = pl.program_id(2)
    ki = pl.program_id(3)

    @pl.when(ki == 0)
    def _():
        m_ref[...] = jnp.full_like(m_ref, MASKED)
        l_ref[...] = jnp.zeros_like(l_ref)
        acc_ref[...] = jnp.zeros_like(acc_ref)

    @pl.when(ki <= qi)
    def _():
        t_pos = qi * tq + lax.broadcasted_iota(jnp.int32, (tq, tk), 0)
        s_pos = ki * tk + lax.broadcasted_iota(jnp.int32, (tq, tk), 1)
        mask = s_pos <= t_pos
        if mode == 'dsa':
            mask = mask & (sc_ref[...] >= thr_ref[:, 0:1])
        kb = k_ref[...].astype(BF16)
        vb = v_ref[...].astype(BF16)
        for g in range(2):
            q = q_ref[:, g * HEAD_DIM:(g + 1) * HEAD_DIM].astype(BF16)
            s = _dot_nt(q, kb) * HEAD_DIM ** -0.5
            if mode == 'fox':
                s = s + (cq_ref[:, g:g + 1] - ck_ref[g:g + 1, :])
            else:
                s = s + bias_ref[g]
            s = jnp.where(mask, s, MASKED)
            m_old = m_ref[g]
            m_new = jnp.maximum(m_old, jnp.max(s, axis=-1, keepdims=True))
            alpha = jnp.exp(m_old - m_new)
            p = jnp.where(mask, jnp.exp(s - m_new), 0.0)
            l_ref[g] = alpha * l_ref[g] + jnp.sum(p, axis=-1, keepdims=True)
            acc_ref[g] = alpha * acc_ref[g] + _dot(p.astype(BF16), vb)
            m_ref[g] = m_new

    @pl.when(ki == qi)
    def _():
        for g in range(2):
            o_ref[:, g * HEAD_DIM:(g + 1) * HEAD_DIM] = acc_ref[g] / l_ref[g]


def _flash(pc, nseq, t, mode, extra, tq=256):
    tk = tq
    nq = t // tq
    c_q, c_k, c_v = (C_FOX_Q, C_FOX_K, C_FOX_V) if mode == 'fox' else (C_DSA_Q, C_DSA_K, C_DSA_V)
    qb, kb, vb = c_q // 256, c_k // HEAD_DIM, c_v // HEAD_DIM
    in_specs = [pl.BlockSpec((tq, 256), lambda b, h, i, j: (b * nq + i, qb + h)),
                pl.BlockSpec((tk, HEAD_DIM), lambda b, h, i, j: (b * nq + jnp.minimum(i, j), kb + h)),
                pl.BlockSpec((tk, HEAD_DIM), lambda b, h, i, j: (b * nq + jnp.minimum(i, j), vb + h))]
    if mode == 'fox':
        in_specs += [pl.BlockSpec((tq, LANES), lambda b, h, i, j: (b * nq + i, h)),
                     pl.BlockSpec((None, 2, tk), lambda b, h, i, j: (b * N_KV + h, 0, jnp.minimum(i, j)))]
    else:
        in_specs += [pl.BlockSpec((None, 2, None, tq, tk),
                                  lambda b, h, i, j: (h, 0, jnp.clip(i - j, 0, 2), 0, 0)),
                     pl.BlockSpec((tq, tk), lambda b, h, i, j: (b * nq + i, jnp.minimum(i, j))),
                     pl.BlockSpec((tq, LANES), lambda b, h, i, j: (b * nq + i, 0))]
    return pl.pallas_call(
        functools.partial(_flash_kernel, mode=mode, tq=tq, tk=tk),
        grid=(nseq, N_KV, nq, nq),
        in_specs=in_specs,
        out_specs=pl.BlockSpec((tq, 256), lambda b, h, i, j: (b * nq + i, h)),
        out_shape=jax.ShapeDtypeStruct((nseq * t, N_HEADS * HEAD_DIM), F32),
        scratch_shapes=[pltpu.VMEM((2, tq, 1), F32), pltpu.VMEM((2, tq, 1), F32),
                        pltpu.VMEM((2, tq, HEAD_DIM), F32)],
        compiler_params=_cparams("parallel", "parallel", "parallel", "arbitrary"),
        name="flash_" + mode,
    )(pc, pc, pc, *extra)


def _split2(x):
    hi = x.astype(BF16)
    return hi, (x - hi.astype(F32)).astype(BF16)


def _sb_kernel(q_ref, k_ref, v_ref, o_ref, r_ref, acc_ref, *, tq, tk):
    qi = pl.program_id(2)
    ki = pl.program_id(3)

    @pl.when(ki == 0)
    def _():
        r_ref[...] = jnp.zeros_like(r_ref)
        acc_ref[...] = jnp.zeros_like(acc_ref)

    @pl.when(ki <= qi)
    def _():
        t_pos = qi * tq + lax.broadcasted_iota(jnp.int32, (tq, tk), 0)
        s_pos = (qi - ki) * tk + lax.broadcasted_iota(jnp.int32, (tq, tk), 1)
        mask = s_pos < t_pos
        after = _tri(tk, lambda r, c: r > c)
        kb = k_ref[...].astype(BF16)
        vb = v_ref[...].astype(BF16)
        for g in range(2):
            q = q_ref[:, g * HEAD_DIM:(g + 1) * HEAD_DIM].astype(BF16)
            z = _dot_nt(q, kb) * HEAD_DIM ** -0.5
            sp = jnp.log(1.0 + jnp.exp(-jnp.abs(z)))
            log_take = jnp.minimum(z, 0.0) - sp
            log_stay = jnp.where(mask, jnp.minimum(-z, 0.0) - sp, 0.0)
            hi, lo = _split2(log_stay)
            later = _dot(hi, after) + _dot(lo, after) + r_ref[g]
            a = jnp.where(mask, jnp.exp(log_take + later), 0.0)
            acc_ref[g] += _dot(a.astype(BF16), vb)
            r_ref[g] += jnp.sum(log_stay, axis=-1, keepdims=True)

    @pl.when(ki == qi)
    def _():
        for g in range(2):
            o_ref[:, g * HEAD_DIM:(g + 1) * HEAD_DIM] = acc_ref[g]


def _sb_attention(pc, nseq, t, tq=256):
    tk = tq
    nq = t // tq
    qb, kb, vb = C_SB_Q // 256, C_SB_K // HEAD_DIM, C_SB_V // HEAD_DIM
    return pl.pallas_call(
        functools.partial(_sb_kernel, tq=tq, tk=tk),
        grid=(nseq, N_KV, nq, nq),
        in_specs=[pl.BlockSpec((tq, 256), lambda b, h, i, j: (b * nq + i, qb + h)),
                  pl.BlockSpec((tk, HEAD_DIM), lambda b, h, i, j: (b * nq + jnp.maximum(i - j, 0), kb + h)),
                  pl.BlockSpec((tk, HEAD_DIM), lambda b, h, i, j: (b * nq + jnp.maximum(i - j, 0), vb + h))],
        out_specs=pl.BlockSpec((tq, 256), lambda b, h, i, j: (b * nq + i, h)),
        out_shape=jax.ShapeDtypeStruct((nseq * t, N_HEADS * HEAD_DIM), F32),
        scratch_shapes=[pltpu.VMEM((2, tq, 1), F32), pltpu.VMEM((2, tq, HEAD_DIM), F32)],
        compiler_params=_cparams("parallel", "parallel", "parallel", "arbitrary"),
        name="sb_attention",
    )(pc, pc, pc)


KEY_NEG_INF = -2139095041


def _float_key(x):
    b = pltpu.bitcast(x, jnp.int32)
    return b ^ ((b >> 31) & 0x7FFFFFFF)


def _key_float(k):
    return pltpu.bitcast(k ^ ((k >> 31) & 0x7FFFFFFF), F32)


def _kth_largest_key(count_ge, rows, k):
    def body(i, res):
        bit = 31 - i
        cand = jnp.where(bit == 31, jnp.zeros_like(res), res | jnp.left_shift(1, jnp.minimum(bit, 30)))
        return jnp.where(count_ge(cand) >= k, cand, res)

    res = jnp.full((rows, 1), jnp.iinfo(jnp.int32).min, jnp.int32)
    return lax.fori_loop(0, 32, body, res)


def _dsa_score_kernel(qi_ref, ki_ref, w_ref, sc_ref, thr_ref, key_ref, *, tq, tk, topk):
    qi = pl.program_id(1)
    ki = pl.program_id(2)

    @pl.when(ki <= qi)
    def _():
        kidx = ki_ref[...].astype(BF16)
        score = jnp.zeros((tq, tk), F32)
        for h in range(IDX_HEADS):
            q = qi_ref[:, h * IDX_DIM:(h + 1) * IDX_DIM].astype(BF16)
            si = _dot_nt(q, kidx) * IDX_DIM ** -0.5
            w = w_ref[:, S_DSA_W + h:S_DSA_W + h + 1]
            score = score + jnp.maximum(si, 0.0) * w
        score = score * IDX_HEADS ** -0.5
        t_pos = qi * tq + lax.broadcasted_iota(jnp.int32, (tq, tk), 0)
        s_pos = ki * tk + lax.broadcasted_iota(jnp.int32, (tq, tk), 1)
        score = jnp.where(s_pos <= t_pos, score, NEG_INF)
        sc_ref[...] = score
        key_ref[ki] = _float_key(score)

    @pl.when(ki == qi)
    def _():
        def count_ge(cand):
            def chunk(c, cnt):
                return cnt + jnp.where(key_ref[c] >= cand, 1, 0)
            cnt = lax.fori_loop(0, qi + 1, chunk, jnp.zeros((tq, tk), jnp.int32))
            return jnp.sum(cnt, axis=-1, keepdims=True)

        res = jnp.maximum(_kth_largest_key(count_ge, tq, topk), KEY_NEG_INF)
        thr_ref[...] = jnp.broadcast_to(_key_float(res), thr_ref.shape)


def _dsa_scores(pc, nseq, t, topk, tq=256):
    tk = tq
    nq = t // tq
    return pl.pallas_call(
        functools.partial(_dsa_score_kernel, tq=tq, tk=tk, topk=topk),
        grid=(nseq, nq, nq),
        in_specs=[pl.BlockSpec((tq, IDX_HEADS * IDX_DIM), lambda b, i, j: (b * nq + i, C_DSA_QIDX // 2048)),
                  pl.BlockSpec((tk, IDX_DIM), lambda b, i, j: (b * nq + jnp.minimum(i, j), C_DSA_KIDX // IDX_DIM)),
                  pl.BlockSpec((tq, LANES), lambda b, i, j: (b * nq + i, C_SMALL // LANES))],
        out_specs=[pl.BlockSpec((tq, tk), lambda b, i, j: (b * nq + i, jnp.minimum(i, j))),
                   pl.BlockSpec((tq, LANES), lambda b, i, j: (b * nq + i, 0))],
        out_shape=[jax.ShapeDtypeStruct((nseq * t, t), F32), jax.ShapeDtypeStruct((nseq * t, LANES), F32)],
        scratch_shapes=[pltpu.VMEM((nq, tq, tk), jnp.int32)],
        compiler_params=_cparams("parallel", "parallel", "arbitrary"),
        name="dsa_scores",
    )(pc, pc, pc)


def _t5_bucket(rel):
    n = jnp.maximum(rel, 0)
    max_exact = N_BUCKETS // 2
    nf = jnp.maximum(n, 1).astype(F32)
    large = max_exact + (jnp.log(nf / max_exact) / math.log(MAX_DISTANCE / max_exact)
                         * (N_BUCKETS - max_exact)).astype(jnp.int32)
    large = jnp.minimum(large, N_BUCKETS - 1)
    return jnp.where(n < max_exact, n, large)


def _rel_bias_tiles(rel_bias, tq):
    i = jnp.arange(tq, dtype=jnp.int32)[:, None]
    j = jnp.arange(tq, dtype=jnp.int32)[None, :]
    rel = jnp.stack([i - j, tq + i - j, 2 * tq + i - j])
    tiles = rel_bias[_t5_bucket(rel)].astype(F32)
    tiles = jnp.transpose(tiles, (3, 0, 1, 2))
    return tiles.reshape(N_KV, 2, 3, tq, tq)


GDN_W = N_HEADS * HEAD_DIM


def _softplus(x):
    return jnp.maximum(x, 0.0) + jnp.log(1.0 + jnp.exp(-jnp.abs(x)))


def _l2norm_heads(x, scale):
    parts = []
    for h in range(N_HEADS):
        xh = x[:, h * HEAD_DIM:(h + 1) * HEAD_DIM]
        parts.append(xh * (lax.rsqrt(jnp.sum(xh * xh, axis=-1, keepdims=True) + EPS) * scale))
    return jnp.concatenate(parts, axis=-1)


def _gdn_gates(small, alog_row, dt_row):
    glog = -jnp.exp(alog_row) * _softplus(small + dt_row)
    beta = 1.0 / (1.0 + jnp.exp(-small))
    lane = lax.broadcasted_iota(jnp.int32, (small.shape[0], LANES), 1)
    blocks = []
    for h in range(N_HEADS):
        g_col = glog[:, S_GDN_A + h:S_GDN_A + h + 1]
        b_col = beta[:, S_GDN_B + h:S_GDN_B + h + 1]
        blocks.append(jnp.where(lane == 0, g_col, jnp.where(lane == 1, b_col, 0.0)))
    return jnp.concatenate(blocks, axis=-1)


def _gdn_prep_prompt_kernel(uq_ref, uk_ref, uv_ref, cw_ref, small_ref, alog_ref, dt_ref,
                            q_ref, k_ref, v_ref, gb_ref, sq_ref, sk_ref, sv_ref, carry_ref):
    @pl.when(pl.program_id(1) == 0)
    def _():
        carry_ref[...] = jnp.zeros_like(carry_ref)

    def conv(idx, u_ref, s_ref):
        u = u_ref[...]
        prev = carry_ref[idx]
        cw = cw_ref[:, idx * GDN_W:(idx + 1) * GDN_W]
        out = cw[GDN_CONV - 1:GDN_CONV, :] * u
        for j in range(GDN_CONV - 1):
            out = out + cw[j:j + 1, :] * _shift_rows(u, prev, GDN_CONV - 1 - j)
        last = u[u.shape[0] - SUBLANES:, :]
        carry_ref[idx] = last
        s_ref[...] = last
        return _silu(out)

    q_ref[...] = _l2norm_heads(conv(0, uq_ref, sq_ref), HEAD_DIM ** -0.5)
    k_ref[...] = _l2norm_heads(conv(1, uk_ref, sk_ref), 1.0)
    v_ref[...] = conv(2, uv_ref, sv_ref)
    gb_ref[...] = _gdn_gates(small_ref[...], alog_ref[...], dt_ref[...])


def _gdn_prep_prompt(pc, conv_w, alog_row, dt_row, nseq, t, tm=256):
    nt = t // tm
    ub = C_GDN_QKV // GDN_W
    row_spec = pl.BlockSpec((tm, GDN_W), lambda b, i: (b * nt + i, 0))
    st_spec = pl.BlockSpec((None, SUBLANES, GDN_W), lambda b, i: (b, 0, 0))
    par_spec = pl.BlockSpec((1, LANES), lambda b, i: (0, 0))
    outs = pl.pallas_call(
        _gdn_prep_prompt_kernel,
        grid=(nseq, nt),
        in_specs=[pl.BlockSpec((tm, GDN_W), lambda b, i: (b * nt + i, ub)),
                  pl.BlockSpec((tm, GDN_W), lambda b, i: (b * nt + i, ub + 1)),
                  pl.BlockSpec((tm, GDN_W), lambda b, i: (b * nt + i, ub + 2)),
                  pl.BlockSpec((GDN_CONV, 3 * GDN_W), lambda b, i: (0, 0)),
                  pl.BlockSpec((tm, LANES), lambda b, i: (b * nt + i, C_SMALL // LANES)),
                  par_spec, par_spec],
        out_specs=[row_spec, row_spec, row_spec, row_spec, st_spec, st_spec, st_spec],
        out_shape=[jax.ShapeDtypeStruct((nseq * t, GDN_W), F32)] * 4
        + [jax.ShapeDtypeStruct((nseq, SUBLANES, GDN_W), F32)] * 3,
        scratch_shapes=[pltpu.VMEM((3, SUBLANES, GDN_W), F32)],
        compiler_params=_cparams("parallel", "arbitrary"),
        name="gdn_prep_prompt",
    )(pc, pc, pc, conv_w, pc, alog_row, dt_row)
    q, k, v, gb, sq, sk, sv = outs
    conv_state = jnp.concatenate([sq, sk, sv], axis=-1)[:, SUBLANES - (GDN_CONV - 1):, :]
    return q, k, v, gb, conv_state


def _dot_f32(a, b):
    return jnp.dot(a, b, precision=lax.Precision.HIGHEST, preferred_element_type=F32)


def _gdn_chunk_kernel(q_ref, k_ref, v_ref, gb_ref, z_ref, nw_ref, o_ref, sfin_ref, s_ref, *, chunks):
    c = GDN_CHUNK

    @pl.when(pl.program_id(2) == 0)
    def _():
        s_ref[...] = jnp.zeros_like(s_ref)

    r = lax.broadcasted_iota(jnp.int32, (c, c), 0)
    col = lax.broadcasted_iota(jnp.int32, (c, c), 1)
    lower = r >= col
    strict = r > col
    eye = jnp.where(r == col, 1.0, 0.0)
    r2 = lax.broadcasted_iota(jnp.int32, (2 * c, c), 0)
    c2 = lax.broadcasted_iota(jnp.int32, (2 * c, c), 1)
    pre_post = jnp.where(((r2 < c) & (c2 <= r2)) | ((r2 >= c) & (c2 > r2 - c)), 1.0, 0.0).astype(BF16)
    low01 = jnp.where(lower, 1.0, 0.0).astype(BF16)

    for ci in range(chunks):
        rows = slice(ci * c, (ci + 1) * c)
        q = q_ref[rows, :]
        k = k_ref[rows, :]
        v = v_ref[rows, :]
        g = gb_ref[rows, 0:1]
        beta = gb_ref[rows, 1:2]
        sums = _dot_exact01(pre_post, jnp.broadcast_to(g, (c, HEAD_DIM)))
        gc = sums[0:c, :]
        rest = sums[c:2 * c, :]
        diff = _dot_exact01(low01, jnp.where(strict, jnp.broadcast_to(g, (c, c)), 0.0))
        decay = jnp.where(lower, jnp.exp(jnp.where(lower, diff, 0.0)), 0.0)
        kb = k * beta
        lmat = jnp.where(strict, _dot_nt(kb.astype(BF16), k.astype(BF16)) * decay, 0.0)
        p = -lmat
        tinv = eye + p
        for _ in range(5):
            p = _dot_f32(p, p)
            tinv = tinv + _dot_f32(tinv, p)
        tb = tinv.astype(BF16)
        u = _dot(tb, (v * beta).astype(BF16))
        w = _dot(tb, (kb * jnp.exp(gc)).astype(BF16))
        aqk = jnp.where(lower, _dot_nt(q.astype(BF16), k.astype(BF16)) * decay, 0.0)
        q_dec = q * jnp.exp(gc)
        k_dec = k * jnp.exp(rest)
        s = s_ref[...]
        sb = s.astype(BF16)
        v_new = u - _dot(w.astype(BF16), sb)
        o = _dot(q_dec.astype(BF16), sb) + _dot(aqk.astype(BF16), v_new.astype(BF16))
        s_ref[...] = s * jnp.exp(gc[c - 1:c, :]) + _dot_tn(k_dec.astype(BF16), v_new.astype(BF16))
        zg = z_ref[rows, :]
        o_ref[rows, :] = _rms(o, nw_ref[...]) * _silu(zg)

    @pl.when(pl.program_id(2) == pl.num_programs(2) - 1)
    def _():
        sfin_ref[...] = s_ref[...]


def _gdn_chunked(q, k, v, gb, pc, norm_w, nseq, t, tm=256):
    nt = t // tm
    zb = C_GDN_Z // HEAD_DIM
    head_spec = pl.BlockSpec((tm, HEAD_DIM), lambda b, h, i: (b * nt + i, h))
    return pl.pallas_call(
        functools.partial(_gdn_chunk_kernel, chunks=tm // GDN_CHUNK),
        grid=(nseq, N_HEADS, nt),
        in_specs=[head_spec, head_spec, head_spec, head_spec,
                  pl.BlockSpec((tm, HEAD_DIM), lambda b, h, i: (b * nt + i, zb + h)),
                  pl.BlockSpec((1, HEAD_DIM), lambda b, h, i: (0, 0))],
        out_specs=[head_spec, pl.BlockSpec((None, None, HEAD_DIM, HEAD_DIM), lambda b, h, i: (b, h, 0, 0))],
        out_shape=[jax.ShapeDtypeStruct((nseq * t, GDN_W), F32),
                   jax.ShapeDtypeStruct((nseq, N_HEADS, HEAD_DIM, HEAD_DIM), F32)],
        scratch_shapes=[pltpu.VMEM((HEAD_DIM, HEAD_DIM), F32)],
        compiler_params=_cparams("parallel", "parallel", "arbitrary"),
        name="gdn_chunked",
    )(q, k, v, gb, pc, norm_w.reshape(1, HEAD_DIM))


def _gdn_prep_sample_kernel(*refs):
    u_refs = refs[0:3]
    p_refs = refs[3:12]
    cw_ref, small_ref, alog_ref, dt_ref, bfox_ref = refs[12:17]
    q_ref, k_ref, v_ref, gb_ref, lf_ref = refs[17:22]

    def conv(idx):
        cw = cw_ref[:, idx * GDN_W:(idx + 1) * GDN_W]
        out = cw[GDN_CONV - 1:GDN_CONV, :] * u_refs[idx][...]
        for r in range(GDN_CONV - 1):
            out = out + cw[r:r + 1, :] * p_refs[3 * r + idx][...]
        return _silu(out)

    q_ref[...] = _l2norm_heads(conv(0), HEAD_DIM ** -0.5)
    k_ref[...] = _l2norm_heads(conv(1), 1.0)
    v_ref[...] = conv(2)
    gb_ref[...] = _gdn_gates(small_ref[...], alog_ref[...], dt_ref[...])
    lf_ref[...] = _log_sigmoid(small_ref[...] + bfox_ref[...])


def _gdn_prep_sample(pc, prev, conv_w, alog_row, dt_row, bfox_row):
    m = pc.shape[0]
    ub = C_GDN_QKV // GDN_W
    prev2 = prev.reshape(m, (GDN_CONV - 1) * 3 * GDN_W)
    row_spec = pl.BlockSpec((m, GDN_W), lambda i: (0, 0))
    par_spec = pl.BlockSpec((1, LANES), lambda i: (0, 0))
    in_specs = [pl.BlockSpec((m, GDN_W), functools.partial(lambda i, c: (0, c), c=ub + n)) for n in range(3)]
    in_specs += [pl.BlockSpec((m, GDN_W), functools.partial(lambda i, c: (0, c), c=n)) for n in range(9)]
    in_specs += [pl.BlockSpec((GDN_CONV, 3 * GDN_W), lambda i: (0, 0)),
                 pl.BlockSpec((m, LANES), lambda i: (0, C_SMALL // LANES)), par_spec, par_spec, par_spec]
    return pl.pallas_call(
        _gdn_prep_sample_kernel,
        grid=(1,),
        in_specs=in_specs,
        out_specs=[row_spec] * 4 + [pl.BlockSpec((m, LANES), lambda i: (0, 0))],
        out_shape=[jax.ShapeDtypeStruct((m, GDN_W), F32)] * 4 + [jax.ShapeDtypeStruct((m, LANES), F32)],
        compiler_params=_cparams("arbitrary"),
        name="gdn_prep_sample",
    )(pc, pc, pc, *([prev2] * 9), conv_w, pc, alog_row, dt_row, bfox_row)


def _gdn_step_kernel(q_ref, k_ref, v_ref, gb_ref, z_ref, nw_ref, s_ref, o_ref, so_ref, *, tb):
    pad = jnp.zeros((HEAD_DIM - tb, HEAD_DIM), F32)
    for h in range(N_HEADS):
        cols = slice(h * HEAD_DIM, (h + 1) * HEAD_DIM)
        k_t = jnp.concatenate([k_ref[:, cols], pad], axis=0).T
        q_t = jnp.concatenate([q_ref[:, cols], pad], axis=0).T
        for b in range(tb):
            decay = jnp.exp(gb_ref[b:b + 1, h * HEAD_DIM:h * HEAD_DIM + 1])
            beta = gb_ref[b:b + 1, h * HEAD_DIM + 1:h * HEAD_DIM + 2]
            kc = k_t[:, b:b + 1]
            s = s_ref[b, h] * decay
            v_old = jnp.sum(s * kc, axis=0, keepdims=True)
            s = s + kc * ((v_ref[b:b + 1, cols] - v_old) * beta)
            so_ref[b, h] = s
            o = jnp.sum(s * q_t[:, b:b + 1], axis=0, keepdims=True)
            o_ref[b:b + 1, cols] = _rms(o, nw_ref[...]) * _silu(z_ref[b:b + 1, cols])


def _gdn_step(q, k, v, gb, pc, norm_w, state, tb=8):
    m = q.shape[0]
    row_spec = pl.BlockSpec((tb, GDN_W), lambda i: (i, 0))
    st_spec = pl.BlockSpec((tb, N_HEADS, HEAD_DIM, HEAD_DIM), lambda i: (i, 0, 0, 0))
    return pl.pallas_call(
        functools.partial(_gdn_step_kernel, tb=tb),
        grid=(m // tb,),
        in_specs=[row_spec, row_spec, row_spec, row_spec,
                  pl.BlockSpec((tb, GDN_W), lambda i: (i, C_GDN_Z // GDN_W)),
                  pl.BlockSpec((1, HEAD_DIM), lambda i: (0, 0)), st_spec],
        out_specs=[row_spec, st_spec],
        out_shape=[jax.ShapeDtypeStruct((m, GDN_W), F32), jax.ShapeDtypeStruct(state.shape, F32)],
        compiler_params=_cparams("parallel"),
        name="gdn_step",
    )(q, k, v, gb, pc, norm_w.reshape(1, HEAD_DIM), state)


def _dec_idx_kernel(pt_ref, q_ref, w_ref, knew_ref, kidx_ref, sc_ref, self_ref):
    q = q_ref[...]
    w = w_ref[...] * (IDX_DIM ** -0.5 * IDX_HEADS ** -0.5)
    si = _dot_nt(q.astype(BF16), kidx_ref[...].astype(BF16))
    sc_ref[...] = jnp.sum(jnp.maximum(si, 0.0) * w, axis=0, keepdims=True)
    si_new = jnp.sum(q * knew_ref[...], axis=-1, keepdims=True)
    own = jnp.sum(jnp.maximum(si_new, 0.0) * w, axis=0, keepdims=True)
    self_ref[...] = jnp.broadcast_to(own, self_ref.shape)


def _dec_idx_scores(layer, page_table, qidx3, w3, knew3, cache_kidx):
    nb, npages = page_table.shape
    page = cache_kidx.shape[2]
    grid_spec = pltpu.PrefetchScalarGridSpec(
        num_scalar_prefetch=1,
        grid=(nb, npages),
        in_specs=[pl.BlockSpec((None, IDX_HEADS, IDX_DIM), lambda b, p, pt: (b, 0, 0)),
                  pl.BlockSpec((None, IDX_HEADS, 1), lambda b, p, pt: (b, 0, 0)),
                  pl.BlockSpec((None, 1, IDX_DIM), lambda b, p, pt: (b, 0, 0)),
                  pl.BlockSpec((None, None, page, IDX_DIM), lambda b, p, pt: (layer, pt[b, p], 0, 0))],
        out_specs=[pl.BlockSpec((None, 1, page), lambda b, p, pt: (b, 0, p)),
                   pl.BlockSpec((None, 1, LANES), lambda b, p, pt: (b, 0, 0))])
    return pl.pallas_call(
        _dec_idx_kernel,
        grid_spec=grid_spec,
        out_shape=[jax.ShapeDtypeStruct((nb, 1, npages * page), F32), jax.ShapeDtypeStruct((nb, 1, LANES), F32)],
        compiler_params=_cparams("parallel", "arbitrary"),
        name="dec_idx_scores",
    )(page_table, qidx3, w3, knew3, cache_kidx)


def _dec_thr_kernel(sc_ref, self_ref, thr_ref, *, topk):
    keys = _float_key(sc_ref[...])
    own = _float_key(self_ref[:, 0:1])

    def count_ge(cand):
        cnt = jnp.sum(jnp.where(keys >= cand, 1, 0), axis=-1, keepdims=True)
        return cnt + jnp.where(own >= cand, 1, 0)

    res = jnp.maximum(_kth_largest_key(count_ge, keys.shape[0], topk), KEY_NEG_INF)
    thr_ref[...] = jnp.broadcast_to(_key_float(res), thr_ref.shape)


def _dec_threshold(scores, own, topk):
    nb, n = scores.shape
    return pl.pallas_call(
        functools.partial(_dec_thr_kernel, topk=topk),
        grid=(1,),
        in_specs=[pl.BlockSpec((nb, n), lambda i: (0, 0)), pl.BlockSpec((nb, LANES), lambda i: (0, 0))],
        out_specs=pl.BlockSpec((nb, LANES), lambda i: (0, 0)),
        out_shape=jax.ShapeDtypeStruct((nb, LANES), F32),
        compiler_params=_cparams("arbitrary"),
        name="dec_threshold",
    )(scores, own)


def _dec_attn_kernel(pt_ref, fq_ref, sq_ref, dq_ref, fkn_ref, fvn_ref, dkn_ref, dvn_ref, lfn_ref, thr_ref, own_ref,
                     bias0_ref, fk_ref, fv_ref, sk_ref, sv_ref, dk_ref, dv_ref, lf_ref, sc_ref, bias_ref,
                     of_ref, os_ref, od_ref,
                     fm_ref, fl_ref, facc_ref, fcar_ref, sr_ref, sacc_ref, dm_ref, dl_ref, dacc_ref):
    p = pl.program_id(1)
    scale = HEAD_DIM ** -0.5
    top = lax.broadcasted_iota(jnp.int32, (SUBLANES, HEAD_DIM), 0) < 2

    def per_head(x2):
        return jnp.where(top, x2[0:1, :], x2[1:2, :])

    def qk(q, k_page):
        qb = q.astype(BF16)
        s0 = _dot_nt(qb, k_page[:, 0:HEAD_DIM].astype(BF16))
        s1 = _dot_nt(qb, k_page[:, HEAD_DIM:2 * HEAD_DIM].astype(BF16))
        return jnp.where(top, s0, s1) * scale

    def pv(w, v_page):
        wb = w.astype(BF16)
        return jnp.where(top, _dot(wb, v_page[:, 0:HEAD_DIM].astype(BF16)),
                         _dot(wb, v_page[:, HEAD_DIM:2 * HEAD_DIM].astype(BF16)))

    @pl.when(p == 0)
    def _():
        s_own = jnp.sum(fq_ref[...] * per_head(fkn_ref[...]), axis=-1, keepdims=True) * scale
        fm_ref[...] = s_own
        fl_ref[...] = jnp.ones_like(fl_ref)
        facc_ref[...] = per_head(fvn_ref[...])
        fcar_ref[...] = lfn_ref[:, 0:1]
        sr_ref[...] = jnp.zeros_like(sr_ref)
        sacc_ref[...] = jnp.zeros_like(sacc_ref)
        sel = own_ref[:, 0:1] >= thr_ref[:, 0:1]
        d_own = jnp.sum(dq_ref[...] * per_head(dkn_ref[...]), axis=-1, keepdims=True) * scale + bias0_ref[:, 0:1]
        dm_ref[...] = jnp.where(sel, d_own, MASKED)
        dl_ref[...] = jnp.where(sel, jnp.ones_like(d_own), 0.0)
        dacc_ref[...] = jnp.where(sel, per_head(dvn_ref[...]), 0.0)

    after = _tri(HEAD_DIM, lambda r, c: r > c)

    lf = lf_ref[...]
    hi, mid, lo = _split3(lf)
    s = qk(fq_ref[...], fk_ref[...]) + (_dot(hi, after) + _dot(mid, after) + _dot(lo, after) + fcar_ref[...])
    m_new = jnp.maximum(fm_ref[...], jnp.max(s, axis=-1, keepdims=True))
    alpha = jnp.exp(fm_ref[...] - m_new)
    w = jnp.exp(s - m_new)
    fl_ref[...] = alpha * fl_ref[...] + jnp.sum(w, axis=-1, keepdims=True)
    facc_ref[...] = alpha * facc_ref[...] + pv(w, fv_ref[...])
    fm_ref[...] = m_new
    fcar_ref[...] += jnp.sum(lf, axis=-1, keepdims=True)

    z = qk(sq_ref[...], sk_ref[...])
    sp = jnp.log(1.0 + jnp.exp(-jnp.abs(z)))
    log_take = jnp.minimum(z, 0.0) - sp
    log_stay = jnp.minimum(-z, 0.0) - sp
    hi, lo = _split2(log_stay)
    later = _dot(hi, after) + _dot(lo, after) + sr_ref[...]
    sacc_ref[...] += pv(jnp.exp(log_take + later), sv_ref[...])
    sr_ref[...] += jnp.sum(log_stay, axis=-1, keepdims=True)

    mask = jnp.broadcast_to(sc_ref[...] >= thr_ref[...], (SUBLANES, HEAD_DIM))
    s = jnp.where(mask, qk(dq_ref[...], dk_ref[...]) + bias_ref[...], MASKED)
    m_new = jnp.maximum(dm_ref[...], jnp.max(s, axis=-1, keepdims=True))
    alpha = jnp.exp(dm_ref[...] - m_new)
    w = jnp.where(mask, jnp.exp(s - m_new), 0.0)
    dl_ref[...] = alpha * dl_ref[...] + jnp.sum(w, axis=-1, keepdims=True)
    dacc_ref[...] = alpha * dacc_ref[...] + pv(w, dv_ref[...])
    dm_ref[...] = m_new

    @pl.when(p == pl.num_programs(1) - 1)
    def _():
        of_ref[...] = facc_ref[...] / fl_ref[...]
        os_ref[...] = sacc_ref[...]
        od_ref[...] = dacc_ref[...] / dl_ref[...]


def _dec_attention(layer, page_table, q8, new_kv, lfn8, thr3, own3, bias0, caches, lf_cache8, scores3, bias_dec):
    nb, npages = page_table.shape
    page = caches[0].shape[2]
    last = npages - 1
    per_b8 = pl.BlockSpec((None, SUBLANES, HEAD_DIM), lambda b, p, pt: (b, 0, 0))
    per_b2 = pl.BlockSpec((None, N_KV, HEAD_DIM), lambda b, p, pt: (b, 0, 0))
    per_b1 = pl.BlockSpec((None, 1, LANES), lambda b, p, pt: (b, 0, 0))
    page_spec = pl.BlockSpec((None, None, page, N_KV * HEAD_DIM), lambda b, p, pt: (layer, pt[b, last - p], 0, 0))
    in_specs = [per_b8, per_b8, per_b8, per_b2, per_b2, per_b2, per_b2, per_b8, per_b1, per_b1,
                pl.BlockSpec((SUBLANES, LANES), lambda b, p, pt: (0, 0))]
    in_specs += [page_spec] * 6
    in_specs += [pl.BlockSpec((None, None, SUBLANES, page), lambda b, p, pt: (layer, pt[b, last - p], 0, 0)),
                 pl.BlockSpec((None, 1, page), lambda b, p, pt: (b, 0, last - p)),
                 pl.BlockSpec((SUBLANES, page), lambda b, p, pt: (0, last - p))]
    grid_spec = pltpu.PrefetchScalarGridSpec(
        num_scalar_prefetch=1, grid=(nb, npages), in_specs=in_specs,
        out_specs=[per_b8, per_b8, per_b8],
        scratch_shapes=[pltpu.VMEM((SUBLANES, 1), F32), pltpu.VMEM((SUBLANES, 1), F32),
                        pltpu.VMEM((SUBLANES, HEAD_DIM), F32), pltpu.VMEM((SUBLANES, 1), F32),
                        pltpu.VMEM((SUBLANES, 1), F32), pltpu.VMEM((SUBLANES, HEAD_DIM), F32),
                        pltpu.VMEM((SUBLANES, 1), F32), pltpu.VMEM((SUBLANES, 1), F32),
                        pltpu.VMEM((SUBLANES, HEAD_DIM), F32)])
    return pl.pallas_call(
        _dec_attn_kernel,
        grid_spec=grid_spec,
        out_shape=[jax.ShapeDtypeStruct((nb, SUBLANES, HEAD_DIM), F32)] * 3,
        compiler_params=_cparams("parallel", "arbitrary"),
        name="dec_attention",
    )(page_table, *q8, *new_kv, lfn8, thr3, own3, bias0, *caches, lf_cache8, scores3, bias_dec)


def _lane_row(values, offset):
    return jnp.zeros((1, LANES), F32).at[0, offset:offset + values.shape[0]].set(values.astype(F32))


def _layer_params(l, w_mod, b_mod, g_mix_pre, g_mix_post, g_ffn_pre, g_ffn_post, w_in, b_fox_f, gdn_a_log,
                  gdn_dt_bias, gdn_conv_w, gdn_norm_w, w_out, w_up, ffn_conv_w, ffn_conv_b, w_down):
    perm, n_real = _in_proj_permutation(w_in.shape[1])
    w_in_p = jnp.pad(w_in[l][:, perm], ((0, 0), (0, D_IN_PAD - n_real))).astype(BF16)
    return dict(
        w_mod=w_mod[l], b_mod=b_mod[l], g_mix_pre=g_mix_pre[l], g_mix_post=g_mix_post[l],
        g_ffn_pre=g_ffn_pre[l], g_ffn_post=g_ffn_post[l], w_in=w_in_p,
        b_fox_row=_lane_row(b_fox_f[l], S_FOX_F), alog_row=_lane_row(gdn_a_log[l], S_GDN_A),
        dt_row=_lane_row(gdn_dt_bias[l], S_GDN_A), gdn_conv_w=gdn_conv_w[l], gdn_norm_w=gdn_norm_w[l],
        w_out=w_out[l].astype(BF16), w_up=w_up[l].astype(BF16), ffn_conv_w=ffn_conv_w[l],
        ffn_conv_b=ffn_conv_b[l], w_down=w_down[l].astype(BF16))


def _kv_state(pc, col, lead):
    return pc[:, col:col + N_KV * HEAD_DIM].reshape(*lead, N_KV, HEAD_DIM)


def _prompt_layer(x, mod, p, rel_tiles, nseq, t):
    pc = _in_proj(x, p['g_mix_pre'], mod, p['w_in'], t, tm=1024, tn=512)
    lf, cum = _fox_gates(pc, p['b_fox_row'], nseq, t)
    cum4 = cum[:, :N_HEADS]
    cum_cols = jnp.pad(cum4.reshape(nseq * t, N_KV, 2), ((0, 0), (0, 0), (0, LANES - 2)))
    cum_cols = cum_cols.reshape(nseq * t, N_KV * LANES)
    cum_rows = jnp.transpose(cum4.reshape(nseq, t, N_KV, 2), (0, 2, 3, 1)).reshape(nseq * N_KV, 2, t)
    o_fox = _flash(pc, nseq, t, 'fox', (cum_cols, cum_rows))
    o_sb = _sb_attention(pc, nseq, t)
    gq, gk, gv, gb, gdn_conv = _gdn_prep_prompt(pc, p['gdn_conv_w'], p['alog_row'], p['dt_row'], nseq, t)
    o_gdn, gdn_s = _gdn_chunked(gq, gk, gv, gb, pc, p['gdn_norm_w'], nseq, t)
    scores, thr = _dsa_scores(pc, nseq, t, min(TOPK_MAX, t // 4))
    o_dsa = _flash(pc, nseq, t, 'dsa', (rel_tiles, scores, thr))
    x = _out_proj((o_fox, o_sb, o_gdn, o_dsa), p['w_out'], x, p['g_mix_post'], mod, t, tm=256)
    x, ffn_conv = _ffn_prompt(x, p['g_ffn_pre'], mod, p['w_up'], p['ffn_conv_w'], p['ffn_conv_b'], p['w_down'],
                              p['g_ffn_post'], t, tm=512, tf=512)
    lead = (nseq, t)
    states = (_kv_state(pc, C_FOX_K, lead), _kv_state(pc, C_FOX_V, lead), lf[:, :N_HEADS].reshape(nseq, t, N_HEADS),
              _kv_state(pc, C_SB_K, lead), _kv_state(pc, C_SB_V, lead),
              _kv_state(pc, C_DSA_K, lead), _kv_state(pc, C_DSA_V, lead),
              pc[:, C_DSA_KIDX:C_DSA_KIDX + IDX_DIM].reshape(nseq, t, IDX_DIM), gdn_s, gdn_conv, ffn_conv)
    return x, states


def _heads8(x):
    nb = x.shape[0]
    return jnp.pad(x.reshape(nb, N_HEADS, HEAD_DIM), ((0, 0), (0, SUBLANES - N_HEADS), (0, 0)))


def _sample_layer(layer, x, mod, p, caches, lf_cache8, kidx_cache, page_table, bias_dec, bias0,
                  gdn_s, gdn_conv, ffn_conv):
    nb = x.shape[0]
    pc = _in_proj(x, p['g_mix_pre'], mod, p['w_in'], 1, tm=nb, tn=512)

    def cols(c, n):
        return pc[:, c:c + n]

    gq, gk, gv, gb, lf = _gdn_prep_sample(pc, gdn_conv, p['gdn_conv_w'], p['alog_row'], p['dt_row'], p['b_fox_row'])
    o_gdn, gdn_s_new = _gdn_step(gq, gk, gv, gb, pc, p['gdn_norm_w'], gdn_s)
    gdn_conv_new = jnp.concatenate([gdn_conv[:, 1:], cols(C_GDN_QKV, 3 * GDN_W)[:, None]], axis=1)

    qidx3 = cols(C_DSA_QIDX, IDX_HEADS * IDX_DIM).reshape(nb, IDX_HEADS, IDX_DIM)
    w3 = cols(C_SMALL + S_DSA_W, IDX_HEADS).reshape(nb, IDX_HEADS, 1)
    knew3 = cols(C_DSA_KIDX, IDX_DIM).reshape(nb, 1, IDX_DIM)
    scores3, own3 = _dec_idx_scores(layer, page_table, qidx3, w3, knew3, kidx_cache)
    past = scores3.shape[-1]
    thr = _dec_threshold(scores3.reshape(nb, past), own3.reshape(nb, LANES), min(TOPK_MAX, (past + 1) // 4))

    lf4 = lf[:, :N_HEADS]
    lfn8 = jnp.broadcast_to(jnp.pad(lf4, ((0, 0), (0, SUBLANES - N_HEADS)))[:, :, None], (nb, SUBLANES, LANES))
    q8 = (_heads8(cols(C_FOX_Q, 512)), _heads8(cols(C_SB_Q, 512)), _heads8(cols(C_DSA_Q, 512)))
    new_kv = tuple(cols(c, 256).reshape(nb, N_KV, HEAD_DIM) for c in (C_FOX_K, C_FOX_V, C_DSA_K, C_DSA_V))
    o8 = _dec_attention(layer, page_table, q8, new_kv, lfn8, thr.reshape(nb, 1, LANES), own3, bias0, caches,
                        lf_cache8, scores3, bias_dec)
    o_fox, o_sb, o_dsa = (o[:, :N_HEADS].reshape(nb, N_HEADS * HEAD_DIM) for o in o8)

    x = _out_proj((o_fox, o_sb, o_gdn, o_dsa), p['w_out'], x, p['g_mix_post'], mod, 1, tm=nb)
    x, up = _ffn_sample(x, p['g_ffn_pre'], mod, p['w_up'], p['ffn_conv_w'], p['ffn_conv_b'], p['w_down'],
                        p['g_ffn_post'], ffn_conv, tf=512)
    ffn_conv_new = jnp.concatenate([ffn_conv[:, 1:], up[:, None]], axis=1)
    lead = (nb, 1)
    states = (_kv_state(pc, C_FOX_K, lead), _kv_state(pc, C_FOX_V, lead), lf4.reshape(nb, 1, N_HEADS),
              _kv_state(pc, C_SB_K, lead), _kv_state(pc, C_SB_V, lead),
              _kv_state(pc, C_DSA_K, lead), _kv_state(pc, C_DSA_V, lead),
              cols(C_DSA_KIDX, IDX_DIM).reshape(nb, 1, IDX_DIM), gdn_s_new, gdn_conv_new, ffn_conv_new)
    return x, states


def kernel(x_prompt, x_sample, cache_fox_k, cache_fox_v, cache_fox_logf, cache_sb_k, cache_sb_v, cache_dsa_k,
           cache_dsa_v, cache_dsa_kidx, state_gdn_s, state_gdn_conv, state_ffn_conv, page_table, c_prompt,
           c_sample, w_mod, b_mod, g_mix_pre, g_mix_post, g_ffn_pre, g_ffn_post, w_in, b_fox_f, gdn_a_log,
           gdn_dt_bias, gdn_conv_w, gdn_norm_w, rel_bias, w_out, w_up, ffn_conv_w, ffn_conv_b, w_down):
    nseq, t, d = x_prompt.shape
    nsamp = x_sample.shape[0]
    depth = w_in.shape[0]
    xp = x_prompt.reshape(nseq * t, d)
    xs = x_sample.reshape(nsamp, d)
    pad = (-(nseq + nsamp)) % SUBLANES
    c_all = jnp.concatenate([c_prompt, c_sample, jnp.zeros((pad, d), F32)], axis=0)
    rel_tiles = _rel_bias_tiles(rel_bias, 256)

    pool_shape = cache_fox_k.shape[:3]
    caches = tuple(c.reshape(*pool_shape, N_KV * HEAD_DIM)
                   for c in (cache_fox_k, cache_fox_v, cache_sb_k, cache_sb_v, cache_dsa_k, cache_dsa_v))
    lf_cache8 = jnp.pad(jnp.swapaxes(cache_fox_logf, 2, 3), ((0, 0), (0, 0), (0, SUBLANES - N_HEADS), (0, 0)))
    past = page_table.shape[1] * pool_shape[2]
    rel_dec = past - jnp.arange(past, dtype=jnp.int32)
    bias_dec = jnp.pad(rel_bias[_t5_bucket(rel_dec)].astype(F32).T, ((0, SUBLANES - N_HEADS), (0, 0)))
    bias0 = jnp.broadcast_to(jnp.pad(rel_bias[0].astype(F32), (0, SUBLANES - N_HEADS))[:, None], (SUBLANES, LANES))

    st_prompt, st_sample = [], []
    for l in range(depth):
        p = _layer_params(l, w_mod, b_mod, g_mix_pre, g_mix_post, g_ffn_pre, g_ffn_post, w_in, b_fox_f, gdn_a_log,
                          gdn_dt_bias, gdn_conv_w, gdn_norm_w, w_out, w_up, ffn_conv_w, ffn_conv_b, w_down)
        mod = _modulation(c_all, p['w_mod'], p['b_mod'])
        xp, sp = _prompt_layer(xp, mod[:nseq], p, rel_tiles, nseq, t)
        xs, ss = _sample_layer(l, xs, mod[nseq:nseq + nsamp], p, caches, lf_cache8, cache_dsa_kidx, page_table,
                               bias_dec, bias0, state_gdn_s[l], state_gdn_conv[l], state_ffn_conv[l])
        st_prompt.append(sp)
        st_sample.append(ss)
    sp = [jnp.stack(z) for z in zip(*st_prompt)]
    ss = [jnp.stack(z) for z in zip(*st_sample)]
    out = [xp.reshape(nseq, t, d), xs.reshape(nsamp, 1, d)]
    for a, b in zip(sp, ss):
        out += [a, b]
    return tuple(out)
```

```python
import functools
import math

import numpy as np
import jax
import jax.numpy as jnp
from jax import lax
from jax.experimental import pallas as pl
from jax.experimental.pallas import tpu as pltpu

F32 = jnp.float32
BF16 = jnp.bfloat16

HEAD_DIM = 128
N_HEADS = 4
N_KV = 2
IDX_HEADS = 16
IDX_DIM = 128
GDN_CONV = 4
GDN_CHUNK = 64
FFN_CONV = 3
TOPK_MAX = 256
N_BUCKETS = 32
MAX_DISTANCE = 128
EPS = 1e-6
NEG_INF = float("-inf")

LANES = 128
SUBLANES = 8
VMEM_LIMIT = 56 * 1024 * 1024

C_FOX_Q, C_FOX_K, C_FOX_V = 0, 512, 768
C_SB_Q, C_SB_K, C_SB_V = 1024, 1536, 1792
C_GDN_QKV, C_GDN_Z = 2048, 3584
C_DSA_QIDX = 4096
C_DSA_Q, C_DSA_K, C_DSA_V = 6144, 6656, 6912
C_DSA_KIDX, C_SMALL = 7168, 7296
D_IN_PAD = 7680
S_FOX_F, S_GDN_A, S_GDN_B, S_DSA_W = 0, 4, 8, 12


def _in_proj_permutation(d_model):
    gdn_qkv = N_HEADS * 3 * HEAD_DIM
    layout = (
        ('fox_q', 512), ('fox_k', 256), ('fox_v', 256), ('fox_f', 4),
        ('sb_q', 512), ('sb_k', 256), ('sb_v', 256),
        ('gdn_qkv', gdn_qkv), ('gdn_a', 4), ('gdn_b', 4), ('gdn_z', 512),
        ('dsa_q', 512), ('dsa_k', 256), ('dsa_v', 256),
        ('dsa_qidx', IDX_HEADS * IDX_DIM), ('dsa_kidx', IDX_DIM), ('dsa_w', IDX_HEADS),
    )
    off = {}
    o = 0
    for name, size in layout:
        off[name] = (o, size)
        o += size
    order = ('fox_q', 'fox_k', 'fox_v', 'sb_q', 'sb_k', 'sb_v', 'gdn_qkv', 'gdn_z', 'dsa_qidx', 'dsa_q', 'dsa_k',
             'dsa_v', 'dsa_kidx', 'fox_f', 'gdn_a', 'gdn_b', 'dsa_w')
    idx = np.concatenate([np.arange(off[n][0], off[n][0] + off[n][1]) for n in order])
    return idx, o


def _cparams(*sem):
    return pltpu.CompilerParams(dimension_semantics=sem, vmem_limit_bytes=VMEM_LIMIT)


def _rms(x, g):
    return x * lax.rsqrt(jnp.mean(x * x, axis=-1, keepdims=True) + EPS) * g


def _silu(x):
    return x * (1.0 / (1.0 + jnp.exp(-x)))


def _log_sigmoid(x):
    return jnp.minimum(x, 0.0) - jnp.log(1.0 + jnp.exp(-jnp.abs(x)))


def _split3(x):
    hi = x.astype(BF16)
    r1 = x - hi.astype(F32)
    mid = r1.astype(BF16)
    lo = (r1 - mid.astype(F32)).astype(BF16)
    return hi, mid, lo


def _dot(a, b):
    return jnp.dot(a, b, preferred_element_type=F32)


def _dot_nt(a, b):
    return lax.dot_general(a, b, (((1,), (1,)), ((), ())), preferred_element_type=F32)


def _dot_tn(a, b):
    return lax.dot_general(a, b, (((0,), (0,)), ((), ())), preferred_element_type=F32)


def _mod_kernel(c_ref, w_ref, b_ref, o_ref):
    a = _silu(c_ref[...]).astype(BF16)
    o_ref[...] = _dot(a, w_ref[...].astype(BF16)) + b_ref[...]


def _modulation(c, w_mod, b_mod):
    m, d = c.shape
    n = w_mod.shape[1]
    tn = 1024
    return pl.pallas_call(
        _mod_kernel,
        grid=(n // tn,),
        in_specs=[pl.BlockSpec((m, d), lambda j: (0, 0)),
                  pl.BlockSpec((d, tn), lambda j: (0, j)),
                  pl.BlockSpec((1, tn), lambda j: (0, j))],
        out_specs=pl.BlockSpec((m, tn), lambda j: (0, j)),
        out_shape=jax.ShapeDtypeStruct((m, n), F32),
        compiler_params=_cparams("parallel"),
        name="modulation",
    )(c, w_mod, b_mod.reshape(1, n))


def _mod_spec(mod, chunk, tm, rows_per_seq, d):
    if rows_per_seq == 1:
        return mod, pl.BlockSpec((tm, d), lambda i, *_: (i, chunk))
    tiles_per_seq = rows_per_seq // tm
    return (mod.reshape(mod.shape[0], 1, mod.shape[1]),
            pl.BlockSpec((None, 1, d), lambda i, *_: (i // tiles_per_seq, 0, chunk)))


def _in_proj_kernel(x_ref, g_ref, sc_ref, sh_ref, w_ref, o_ref, h_ref):
    @pl.when(pl.program_id(1) == 0)
    def _():
        h = _rms(x_ref[...], g_ref[...]) * (1.0 + sc_ref[...]) + sh_ref[...]
        h_ref[...] = h.astype(BF16)

    o_ref[...] = _dot(h_ref[...], w_ref[...])


def _in_proj(x, g, mod, w, rows_per_seq, tm, tn):
    m, d = x.shape
    n = w.shape[1]
    sc, sc_spec = _mod_spec(mod, 1, tm, rows_per_seq, d)
    sh, sh_spec = _mod_spec(mod, 0, tm, rows_per_seq, d)
    return pl.pallas_call(
        _in_proj_kernel,
        grid=(m // tm, n // tn),
        in_specs=[pl.BlockSpec((tm, d), lambda i, j: (i, 0)),
                  pl.BlockSpec((1, d), lambda i, j: (0, 0)),
                  sc_spec, sh_spec,
                  pl.BlockSpec((d, tn), lambda i, j: (0, j))],
        out_specs=pl.BlockSpec((tm, tn), lambda i, j: (i, j)),
        out_shape=jax.ShapeDtypeStruct((m, n), F32),
        scratch_shapes=[pltpu.VMEM((tm, d), BF16)],
        compiler_params=_cparams("parallel", "arbitrary"),
        name="in_proj",
    )(x, g.reshape(1, d), sc, sh, w)


def _out_proj_kernel(a0_ref, a1_ref, a2_ref, a3_ref, w_ref, x_ref, g_ref, gate_ref, o_ref):
    m = None
    for n, a_ref in enumerate((a0_ref, a1_ref, a2_ref, a3_ref)):
        k = a_ref.shape[1]
        part = _dot(a_ref[...].astype(BF16), w_ref[n * k:(n + 1) * k, :])
        m = part if m is None else m + part
    o_ref[...] = x_ref[...] + gate_ref[...] * _rms(m, g_ref[...])


def _out_proj(mixed, w, x, g, mod, rows_per_seq, tm):
    m, k4 = mixed[0].shape
    k, d = w.shape
    gate, gate_spec = _mod_spec(mod, 2, tm, rows_per_seq, d)
    a_spec = pl.BlockSpec((tm, k4), lambda i: (i, 0))
    return pl.pallas_call(
        _out_proj_kernel,
        grid=(m // tm,),
        in_specs=[a_spec, a_spec, a_spec, a_spec,
                  pl.BlockSpec((k, d), lambda i: (0, 0)),
                  pl.BlockSpec((tm, d), lambda i: (i, 0)),
                  pl.BlockSpec((1, d), lambda i: (0, 0)),
                  gate_spec],
        out_specs=pl.BlockSpec((tm, d), lambda i: (i, 0)),
        out_shape=jax.ShapeDtypeStruct((m, d), F32),
        compiler_params=_cparams("parallel"),
        name="out_proj",
    )(*mixed, w, x, g.reshape(1, d), gate)


def _shift_rows(u, prev, k):
    rolled = pltpu.roll(u, k, axis=0)
    row = lax.broadcasted_iota(jnp.int32, u.shape, 0)
    out = rolled
    for r in range(k):
        out = jnp.where(row == r, prev[SUBLANES - k + r:SUBLANES - k + r + 1, :], out)
    return out


def _ffn_prompt_kernel(x_ref, g_ref, sc_ref, sh_ref, wg_ref, wv_ref, cwg_ref, cwv_ref, cbg_ref, cbv_ref,
                       wd_ref, gpost_ref, gate_ref, o_ref, sg_ref, sv_ref,
                       h_ref, acc_ref, cg_ref, cv_ref, *, tiles_per_seq):
    i = pl.program_id(0)
    j = pl.program_id(1)

    @pl.when(j == 0)
    def _():
        h = _rms(x_ref[...], g_ref[...]) * (1.0 + sc_ref[...]) + sh_ref[...]
        h_ref[...] = h.astype(BF16)
        acc_ref[...] = jnp.zeros_like(acc_ref)

    fresh = i % tiles_per_seq == 0

    def conv(u, carry_ref, cw_ref, cb_ref):
        prev = jnp.where(fresh, 0.0, carry_ref[j])
        out = cw_ref[0:1, :] * _shift_rows(u, prev, 2) + cw_ref[1:2, :] * _shift_rows(u, prev, 1)
        out = out + cw_ref[2:3, :] * u + cb_ref[...]
        carry_ref[j] = u[u.shape[0] - SUBLANES:, :]
        return out

    ug = _dot(h_ref[...], wg_ref[...])
    uv = _dot(h_ref[...], wv_ref[...])
    sg_ref[...] = ug[ug.shape[0] - SUBLANES:, :]
    sv_ref[...] = uv[uv.shape[0] - SUBLANES:, :]
    gate = conv(ug, cg_ref, cwg_ref, cbg_ref)
    val = conv(uv, cv_ref, cwv_ref, cbv_ref)
    hid = (_silu(gate) * val).astype(BF16)
    acc_ref[...] += _dot(hid, wd_ref[...])

    @pl.when(j == pl.num_programs(1) - 1)
    def _():
        o_ref[...] = x_ref[...] + gate_ref[...] * _rms(acc_ref[...], gpost_ref[...])


def _ffn_prompt(x, g_pre, mod, w_up, conv_w, conv_b, w_down, g_post, rows_per_seq, tm, tf):
    m, d = x.shape
    f = w_down.shape[0]
    nf = f // tf
    nseq = m // rows_per_seq
    tiles_per_seq = rows_per_seq // tm
    sc, sc_spec = _mod_spec(mod, 4, tm, rows_per_seq, d)
    sh, sh_spec = _mod_spec(mod, 3, tm, rows_per_seq, d)
    gate, gate_spec = _mod_spec(mod, 5, tm, rows_per_seq, d)
    state_spec = pl.BlockSpec((None, SUBLANES, tf), lambda i, j: (i, 0, j))
    out, sg, sv = pl.pallas_call(
        functools.partial(_ffn_prompt_kernel, tiles_per_seq=tiles_per_seq),
        grid=(m // tm, nf),
        in_specs=[pl.BlockSpec((tm, d), lambda i, j: (i, 0)),
                  pl.BlockSpec((1, d), lambda i, j: (0, 0)),
                  sc_spec, sh_spec,
                  pl.BlockSpec((d, tf), lambda i, j: (0, j)),
                  pl.BlockSpec((d, tf), lambda i, j: (0, j + nf)),
                  pl.BlockSpec((FFN_CONV, tf), lambda i, j: (0, j)),
                  pl.BlockSpec((FFN_CONV, tf), lambda i, j: (0, j + nf)),
                  pl.BlockSpec((1, tf), lambda i, j: (0, j)),
                  pl.BlockSpec((1, tf), lambda i, j: (0, j + nf)),
                  pl.BlockSpec((tf, d), lambda i, j: (j, 0)),
                  pl.BlockSpec((1, d), lambda i, j: (0, 0)),
                  gate_spec],
        out_specs=[pl.BlockSpec((tm, d), lambda i, j: (i, 0)), state_spec, state_spec],
        out_shape=[jax.ShapeDtypeStruct((m, d), F32),
                   jax.ShapeDtypeStruct((m // tm, SUBLANES, f), F32),
                   jax.ShapeDtypeStruct((m // tm, SUBLANES, f), F32)],
        scratch_shapes=[pltpu.VMEM((tm, d), BF16), pltpu.VMEM((tm, d), F32),
                        pltpu.VMEM((nf, SUBLANES, tf), F32), pltpu.VMEM((nf, SUBLANES, tf), F32)],
        compiler_params=_cparams("arbitrary", "arbitrary"),
        name="ffn_prompt",
    )(x, g_pre.reshape(1, d), sc, sh, w_up, w_up, conv_w, conv_w, conv_b.reshape(1, 2 * f),
      conv_b.reshape(1, 2 * f), w_down, g_post.reshape(1, d), gate)
    state = jnp.concatenate([sg, sv], axis=-1)[tiles_per_seq - 1::tiles_per_seq, SUBLANES - (FFN_CONV - 1):, :]
    return out, state


def _ffn_sample_kernel(x_ref, g_ref, sc_ref, sh_ref, wg_ref, wv_ref, cwg_ref, cwv_ref, cbg_ref, cbv_ref,
                       p0g_ref, p0v_ref, p1g_ref, p1v_ref, wd_ref, gpost_ref, gate_ref,
                       o_ref, ug_ref, uv_ref, h_ref, acc_ref):
    j = pl.program_id(1)

    @pl.when(j == 0)
    def _():
        h = _rms(x_ref[...], g_ref[...]) * (1.0 + sc_ref[...]) + sh_ref[...]
        h_ref[...] = h.astype(BF16)
        acc_ref[...] = jnp.zeros_like(acc_ref)

    ug = _dot(h_ref[...], wg_ref[...])
    uv = _dot(h_ref[...], wv_ref[...])
    ug_ref[...] = ug
    uv_ref[...] = uv
    gate = cwg_ref[0:1, :] * p0g_ref[...] + cwg_ref[1:2, :] * p1g_ref[...] + cwg_ref[2:3, :] * ug + cbg_ref[...]
    val = cwv_ref[0:1, :] * p0v_ref[...] + cwv_ref[1:2, :] * p1v_ref[...] + cwv_ref[2:3, :] * uv + cbv_ref[...]
    hid = (_silu(gate) * val).astype(BF16)
    acc_ref[...] += _dot(hid, wd_ref[...])

    @pl.when(j == pl.num_programs(1) - 1)
    def _():
        o_ref[...] = x_ref[...] + gate_ref[...] * _rms(acc_ref[...], gpost_ref[...])


def _ffn_sample(x, g_pre, mod, w_up, conv_w, conv_b, w_down, g_post, prev, tf):
    m, d = x.shape
    f = w_down.shape[0]
    nf = f // tf
    tm = m
    sc, sc_spec = _mod_spec(mod, 4, tm, 1, d)
    sh, sh_spec = _mod_spec(mod, 3, tm, 1, d)
    gate, gate_spec = _mod_spec(mod, 5, tm, 1, d)
    prev2 = prev.reshape(m, (FFN_CONV - 1) * 2 * f)
    up_spec = pl.BlockSpec((tm, tf), lambda i, j: (i, j))
    out, up_g, up_v = pl.pallas_call(
        _ffn_sample_kernel,
        grid=(1, nf),
        in_specs=[pl.BlockSpec((tm, d), lambda i, j: (i, 0)),
                  pl.BlockSpec((1, d), lambda i, j: (0, 0)),
                  sc_spec, sh_spec,
                  pl.BlockSpec((d, tf), lambda i, j: (0, j)),
                  pl.BlockSpec((d, tf), lambda i, j: (0, j + nf)),
                  pl.BlockSpec((FFN_CONV, tf), lambda i, j: (0, j)),
                  pl.BlockSpec((FFN_CONV, tf), lambda i, j: (0, j + nf)),
                  pl.BlockSpec((1, tf), lambda i, j: (0, j)),
                  pl.BlockSpec((1, tf), lambda i, j: (0, j + nf)),
                  pl.BlockSpec((tm, tf), lambda i, j: (i, j)),
                  pl.BlockSpec((tm, tf), lambda i, j: (i, j + nf)),
                  pl.BlockSpec((tm, tf), lambda i, j: (i, j + 2 * nf)),
                  pl.BlockSpec((tm, tf), lambda i, j: (i, j + 3 * nf)),
                  pl.BlockSpec((tf, d), lambda i, j: (j, 0)),
                  pl.BlockSpec((1, d), lambda i, j: (0, 0)),
                  gate_spec],
        out_specs=[pl.BlockSpec((tm, d), lambda i, j: (i, 0)), up_spec, up_spec],
        out_shape=[jax.ShapeDtypeStruct((m, d), F32), jax.ShapeDtypeStruct((m, f), F32),
                   jax.ShapeDtypeStruct((m, f), F32)],
        scratch_shapes=[pltpu.VMEM((tm, d), BF16), pltpu.VMEM((tm, d), F32)],
        compiler_params=_cparams("arbitrary", "arbitrary"),
        name="ffn_sample",
    )(x, g_pre.reshape(1, d), sc, sh, w_up, w_up, conv_w, conv_w, conv_b.reshape(1, 2 * f),
      conv_b.reshape(1, 2 * f), prev2, prev2, prev2, prev2, w_down, g_post.reshape(1, d), gate)
    return out, jnp.concatenate([up_g, up_v], axis=-1)


def _tri(n, fn):
    r = lax.broadcasted_iota(jnp.int32, (n, n), 0)
    c = lax.broadcasted_iota(jnp.int32, (n, n), 1)
    return jnp.where(fn(r, c), 1.0, 0.0).astype(BF16)


def _dot_exact01(m01, x):
    hi, mid, lo = _split3(x)
    return _dot(m01, hi) + _dot(m01, mid) + _dot(m01, lo)


def _fox_gate_kernel(s_ref, b_ref, lf_ref, cum_ref, carry_ref):
    @pl.when(pl.program_id(1) == 0)
    def _():
        carry_ref[...] = jnp.zeros_like(carry_ref)

    lf = _log_sigmoid(s_ref[...] + b_ref[...])
    n = lf.shape[0]
    cum = _dot_exact01(_tri(n, lambda r, c: r >= c), lf) + carry_ref[0:1, :]
    lf_ref[...] = lf
    cum_ref[...] = cum
    carry_ref[...] = jnp.broadcast_to(cum[n - 1:n, :], carry_ref.shape)


def _fox_gates(pc, b_row, nseq, t, tc=256):
    nt = t // tc
    cb = C_SMALL // LANES
    spec = pl.BlockSpec((tc, LANES), lambda b, i: (b * nt + i, 0))
    return pl.pallas_call(
        _fox_gate_kernel,
        grid=(nseq, nt),
        in_specs=[pl.BlockSpec((tc, LANES), lambda b, i: (b * nt + i, cb)),
                  pl.BlockSpec((1, LANES), lambda b, i: (0, 0))],
        out_specs=[spec, spec],
        out_shape=[jax.ShapeDtypeStruct((nseq * t, LANES), F32)] * 2,
        scratch_shapes=[pltpu.VMEM((SUBLANES, LANES), F32)],
        compiler_params=_cparams("parallel", "arbitrary"),
        name="fox_gates",
    )(pc, b_row)


MASKED = -1e30


def _flash_kernel(*refs, mode, tq, tk):
    if mode == 'fox':
        q_ref, k_ref, v_ref, cq_ref, ck_ref, o_ref, m_ref, l_ref, acc_ref = refs
    else:
        q_ref, k_ref, v_ref, bias_ref, sc_ref, thr_ref, o_ref, m_ref, l_ref, acc_ref = refs
    qi = pl.program_id(2)
    ki = pl.program_id(3)

    @pl.when(ki == 0)
    def _():
        m_ref[...] = jnp.full_like(m_ref, MASKED)
        l_ref[...] = jnp.zeros_like(l_ref)
        acc_ref[...] = jnp.zeros_like(acc_ref)

    @pl.when(ki <= qi)
    def _():
        t_pos = qi * tq + lax.broadcasted_iota(jnp.int32, (tq, tk), 0)
        s_pos = ki * tk + lax.broadcasted_iota(jnp.int32, (tq, tk), 1)
        mask = s_pos <= t_pos
        if mode == 'dsa':
            mask = mask & (sc_ref[...] >= thr_ref[:, 0:1])
        kb = k_ref[...].astype(BF16)
        vb = v_ref[...].astype(BF16)
        for g in range(2):
            q = q_ref[:, g * HEAD_DIM:(g + 1) * HEAD_DIM].astype(BF16)
            s = _dot_nt(q, kb) * HEAD_DIM ** -0.5
            if mode == 'fox':
                s = s + (cq_ref[:, g:g + 1] - ck_ref[g:g + 1, :])
            else:
                s = s + bias_ref[g]
            s = jnp.where(mask, s, MASKED)
            m_old = m_ref[g]
            m_new = jnp.maximum(m_old, jnp.max(s, axis=-1, keepdims=True))
            alpha = jnp.exp(m_old - m_new)
            p = jnp.where(mask, jnp.exp(s - m_new), 0.0)
            l_ref[g] = alpha * l_ref[g] + jnp.sum(p, axis=-1, keepdims=True)
            acc_ref[g] = alpha * acc_ref[g] + _dot(p.astype(BF16), vb)
            m_ref[g] = m_new

    @pl.when(ki == qi)
    def _():
        for g in range(2):
            o_ref[:, g * HEAD_DIM:(g + 1) * HEAD_DIM] = acc_ref[g] / l_ref[g]


def _flash(pc, nseq, t, mode, extra, tq=256):
    tk = tq
    nq = t // tq
    c_q, c_k, c_v = (C_FOX_Q, C_FOX_K, C_FOX_V) if mode == 'fox' else (C_DSA_Q, C_DSA_K, C_DSA_V)
    qb, kb, vb = c_q // 256, c_k // HEAD_DIM, c_v // HEAD_DIM
    in_specs = [pl.BlockSpec((tq, 256), lambda b, h, i, j: (b * nq + i, qb + h)),
                pl.BlockSpec((tk, HEAD_DIM), lambda b, h, i, j: (b * nq + jnp.minimum(i, j), kb + h)),
                pl.BlockSpec((tk, HEAD_DIM), lambda b, h, i, j: (b * nq + jnp.minimum(i, j), vb + h))]
    if mode == 'fox':
        in_specs += [pl.BlockSpec((tq, LANES), lambda b, h, i, j: (b * nq + i, h)),
                     pl.BlockSpec((None, 2, tk), lambda b, h, i, j: (b * N_KV + h, 0, jnp.minimum(i, j)))]
    else:
        in_specs += [pl.BlockSpec((None, 2, None, tq, tk),
                                  lambda b, h, i, j: (h, 0, jnp.clip(i - j, 0, 2), 0, 0)),
                     pl.BlockSpec((tq, tk), lambda b, h, i, j: (b * nq + i, jnp.minimum(i, j))),
                     pl.BlockSpec((tq, LANES), lambda b, h, i, j: (b * nq + i, 0))]
    return pl.pallas_call(
        functools.partial(_flash_kernel, mode=mode, tq=tq, tk=tk),
        grid=(nseq, N_KV, nq, nq),
        in_specs=in_specs,
        out_specs=pl.BlockSpec((tq, 256), lambda b, h, i, j: (b * nq + i, h)),
        out_shape=jax.ShapeDtypeStruct((nseq * t, N_HEADS * HEAD_DIM), F32),
        scratch_shapes=[pltpu.VMEM((2, tq, 1), F32), pltpu.VMEM((2, tq, 1), F32),
                        pltpu.VMEM((2, tq, HEAD_DIM), F32)],
        compiler_params=_cparams("parallel", "parallel", "parallel", "arbitrary"),
        name="flash_" + mode,
    )(pc, pc, pc, *extra)


def _split2(x):
    hi = x.astype(BF16)
    return hi, (x - hi.astype(F32)).astype(BF16)


def _sb_kernel(q_ref, k_ref, v_ref, o_ref, r_ref, acc_ref, *, tq, tk):
    qi = pl.program_id(2)
    ki = pl.program_id(3)

    @pl.when(ki == 0)
    def _():
        r_ref[...] = jnp.zeros_like(r_ref)
        acc_ref[...] = jnp.zeros_like(acc_ref)

    @pl.when(ki <= qi)
    def _():
        t_pos = qi * tq + lax.broadcasted_iota(jnp.int32, (tq, tk), 0)
        s_pos = (qi - ki) * tk + lax.broadcasted_iota(jnp.int32, (tq, tk), 1)
        mask = s_pos < t_pos
        after = _tri(tk, lambda r, c: r > c)
        kb = k_ref[...].astype(BF16)
        vb = v_ref[...].astype(BF16)
        for g in range(2):
            q = q_ref[:, g * HEAD_DIM:(g + 1) * HEAD_DIM].astype(BF16)
            z = _dot_nt(q, kb) * HEAD_DIM ** -0.5
            sp = jnp.log(1.0 + jnp.exp(-jnp.abs(z)))
            log_take = jnp.minimum(z, 0.0) - sp
            log_stay = jnp.where(mask, jnp.minimum(-z, 0.0) - sp, 0.0)
            hi, lo = _split2(log_stay)
            later = _dot(hi, after) + _dot(lo, after) + r_ref[g]
            a = jnp.where(mask, jnp.exp(log_take + later), 0.0)
            acc_ref[g] += _dot(a.astype(BF16), vb)
            r_ref[g] += jnp.sum(log_stay, axis=-1, keepdims=True)

    @pl.when(ki == qi)
    def _():
        for g in range(2):
            o_ref[:, g * HEAD_DIM:(g + 1) * HEAD_DIM] = acc_ref[g]


def _sb_attention(pc, nseq, t, tq=256):
    tk = tq
    nq = t // tq
    qb, kb, vb = C_SB_Q // 256, C_SB_K // HEAD_DIM, C_SB_V // HEAD_DIM
    return pl.pallas_call(
        functools.partial(_sb_kernel, tq=tq, tk=tk),
        grid=(nseq, N_KV, nq, nq),
        in_specs=[pl.BlockSpec((tq, 256), lambda b, h, i, j: (b * nq + i, qb + h)),
                  pl.BlockSpec((tk, HEAD_DIM), lambda b, h, i, j: (b * nq + jnp.maximum(i - j, 0), kb + h)),
                  pl.BlockSpec((tk, HEAD_DIM), lambda b, h, i, j: (b * nq + jnp.maximum(i - j, 0), vb + h))],
        out_specs=pl.BlockSpec((tq, 256), lambda b, h, i, j: (b * nq + i, h)),
        out_shape=jax.ShapeDtypeStruct((nseq * t, N_HEADS * HEAD_DIM), F32),
        scratch_shapes=[pltpu.VMEM((2, tq, 1), F32), pltpu.VMEM((2, tq, HEAD_DIM), F32)],
        compiler_params=_cparams("parallel", "parallel", "parallel", "arbitrary"),
        name="sb_attention",
    )(pc, pc, pc)


KEY_NEG_INF = -2139095041


def _float_key(x):
    b = pltpu.bitcast(x, jnp.int32)
    return b ^ ((b >> 31) & 0x7FFFFFFF)


def _key_float(k):
    return pltpu.bitcast(k ^ ((k >> 31) & 0x7FFFFFFF), F32)


def _kth_largest_key(count_ge, rows, k):
    def body(i, res):
        bit = 31 - i
        cand = jnp.where(bit == 31, jnp.zeros_like(res), res | jnp.left_shift(1, jnp.minimum(bit, 30)))
        return jnp.where(count_ge(cand) >= k, cand, res)

    res = jnp.full((rows, 1), jnp.iinfo(jnp.int32).min, jnp.int32)
    return lax.fori_loop(0, 32, body, res)


def _dsa_score_kernel(qi_ref, ki_ref, w_ref, sc_ref, thr_ref, key_ref, *, tq, tk, topk):
    qi = pl.program_id(1)
    ki = pl.program_id(2)

    @pl.when(ki <= qi)
    def _():
        kidx = ki_ref[...].astype(BF16)
        score = jnp.zeros((tq, tk), F32)
        for h in range(IDX_HEADS):
            q = qi_ref[:, h * IDX_DIM:(h + 1) * IDX_DIM].astype(BF16)
            si = _dot_nt(q, kidx) * IDX_DIM ** -0.5
            w = w_ref[:, S_DSA_W + h:S_DSA_W + h + 1]
            score = score + jnp.maximum(si, 0.0) * w
        score = score * IDX_HEADS ** -0.5
        t_pos = qi * tq + lax.broadcasted_iota(jnp.int32, (tq, tk), 0)
        s_pos = ki * tk + lax.broadcasted_iota(jnp.int32, (tq, tk), 1)
        score = jnp.where(s_pos <= t_pos, score, NEG_INF)
        sc_ref[...] = score
        key_ref[ki] = _float_key(score)

    @pl.when(ki == qi)
    def _():
        def count_ge(cand):
            def chunk(c, cnt):
                return cnt + jnp.where(key_ref[c] >= cand, 1, 0)
            cnt = lax.fori_loop(0, qi + 1, chunk, jnp.zeros((tq, tk), jnp.int32))
            return jnp.sum(cnt, axis=-1, keepdims=True)

        res = jnp.maximum(_kth_largest_key(count_ge, tq, topk), KEY_NEG_INF)
        thr_ref[...] = jnp.broadcast_to(_key_float(res), thr_ref.shape)


def _dsa_scores(pc, nseq, t, topk, tq=256):
    tk = tq
    nq = t // tq
    return pl.pallas_call(
        functools.partial(_dsa_score_kernel, tq=tq, tk=tk, topk=topk),
        grid=(nseq, nq, nq),
        in_specs=[pl.BlockSpec((tq, IDX_HEADS * IDX_DIM), lambda b, i, j: (b * nq + i, C_DSA_QIDX // 2048)),
                  pl.BlockSpec((tk, IDX_DIM), lambda b, i, j: (b * nq + jnp.minimum(i, j), C_DSA_KIDX // IDX_DIM)),
                  pl.BlockSpec((tq, LANES), lambda b, i, j: (b * nq + i, C_SMALL // LANES))],
        out_specs=[pl.BlockSpec((tq, tk), lambda b, i, j: (b * nq + i, jnp.minimum(i, j))),
                   pl.BlockSpec((tq, LANES), lambda b, i, j: (b * nq + i, 0))],
        out_shape=[jax.ShapeDtypeStruct((nseq * t, t), F32), jax.ShapeDtypeStruct((nseq * t, LANES), F32)],
        scratch_shapes=[pltpu.VMEM((nq, tq, tk), jnp.int32)],
        compiler_params=_cparams("parallel", "parallel", "arbitrary"),
        name="dsa_scores",
    )(pc, pc, pc)


def _t5_bucket(rel):
    n = jnp.maximum(rel, 0)
    max_exact = N_BUCKETS // 2
    nf = jnp.maximum(n, 1).astype(F32)
    large = max_exact + (jnp.log(nf / max_exact) / math.log(MAX_DISTANCE / max_exact)
                         * (N_BUCKETS - max_exact)).astype(jnp.int32)
    large = jnp.minimum(large, N_BUCKETS - 1)
    return jnp.where(n < max_exact, n, large)


def _rel_bias_tiles(rel_bias, tq):
    i = jnp.arange(tq, dtype=jnp.int32)[:, None]
    j = jnp.arange(tq, dtype=jnp.int32)[None, :]
    rel = jnp.stack([i - j, tq + i - j, 2 * tq + i - j])
    tiles = rel_bias[_t5_bucket(rel)].astype(F32)
    tiles = jnp.transpose(tiles, (3, 0, 1, 2))
    return tiles.reshape(N_KV, 2, 3, tq, tq)


GDN_W = N_HEADS * HEAD_DIM


def _softplus(x):
    return jnp.maximum(x, 0.0) + jnp.log(1.0 + jnp.exp(-jnp.abs(x)))


def _l2norm_heads(x, scale):
    parts = []
    for h in range(N_HEADS):
        xh = x[:, h * HEAD_DIM:(h + 1) * HEAD_DIM]
        parts.append(xh * (lax.rsqrt(jnp.sum(xh * xh, axis=-1, keepdims=True) + EPS) * scale))
    return jnp.concatenate(parts, axis=-1)


def _gdn_gates(small, alog_row, dt_row):
    glog = -jnp.exp(alog_row) * _softplus(small + dt_row)
    beta = 1.0 / (1.0 + jnp.exp(-small))
    lane = lax.broadcasted_iota(jnp.int32, (small.shape[0], LANES), 1)
    blocks = []
    for h in range(N_HEADS):
        g_col = glog[:, S_GDN_A + h:S_GDN_A + h + 1]
        b_col = beta[:, S_GDN_B + h:S_GDN_B + h + 1]
        blocks.append(jnp.where(lane == 0, g_col, jnp.where(lane == 1, b_col, 0.0)))
    return jnp.concatenate(blocks, axis=-1)


def _gdn_prep_prompt_kernel(uq_ref, uk_ref, uv_ref, cw_ref, small_ref, alog_ref, dt_ref,
                            q_ref, k_ref, v_ref, gb_ref, sq_ref, sk_ref, sv_ref, carry_ref):
    @pl.when(pl.program_id(1) == 0)
    def _():
        carry_ref[...] = jnp.zeros_like(carry_ref)

    def conv(idx, u_ref, s_ref):
        u = u_ref[...]
        prev = carry_ref[idx]
        cw = cw_ref[:, idx * GDN_W:(idx + 1) * GDN_W]
        out = cw[GDN_CONV - 1:GDN_CONV, :] * u
        for j in range(GDN_CONV - 1):
            out = out + cw[j:j + 1, :] * _shift_rows(u, prev, GDN_CONV - 1 - j)
        last = u[u.shape[0] - SUBLANES:, :]
        carry_ref[idx] = last
        s_ref[...] = last
        return _silu(out)

    q_ref[...] = _l2norm_heads(conv(0, uq_ref, sq_ref), HEAD_DIM ** -0.5)
    k_ref[...] = _l2norm_heads(conv(1, uk_ref, sk_ref), 1.0)
    v_ref[...] = conv(2, uv_ref, sv_ref)
    gb_ref[...] = _gdn_gates(small_ref[...], alog_ref[...], dt_ref[...])


def _gdn_prep_prompt(pc, conv_w, alog_row, dt_row, nseq, t, tm=256):
    nt = t // tm
    ub = C_GDN_QKV // GDN_W
    row_spec = pl.BlockSpec((tm, GDN_W), lambda b, i: (b * nt + i, 0))
    st_spec = pl.BlockSpec((None, SUBLANES, GDN_W), lambda b, i: (b, 0, 0))
    par_spec = pl.BlockSpec((1, LANES), lambda b, i: (0, 0))
    outs = pl.pallas_call(
        _gdn_prep_prompt_kernel,
        grid=(nseq, nt),
        in_specs=[pl.BlockSpec((tm, GDN_W), lambda b, i: (b * nt + i, ub)),
                  pl.BlockSpec((tm, GDN_W), lambda b, i: (b * nt + i, ub + 1)),
                  pl.BlockSpec((tm, GDN_W), lambda b, i: (b * nt + i, ub + 2)),
                  pl.BlockSpec((GDN_CONV, 3 * GDN_W), lambda b, i: (0, 0)),
                  pl.BlockSpec((tm, LANES), lambda b, i: (b * nt + i, C_SMALL // LANES)),
                  par_spec, par_spec],
        out_specs=[row_spec, row_spec, row_spec, row_spec, st_spec, st_spec, st_spec],
        out_shape=[jax.ShapeDtypeStruct((nseq * t, GDN_W), F32)] * 4
        + [jax.ShapeDtypeStruct((nseq, SUBLANES, GDN_W), F32)] * 3,
        scratch_shapes=[pltpu.VMEM((3, SUBLANES, GDN_W), F32)],
        compiler_params=_cparams("parallel", "arbitrary"),
        name="gdn_prep_prompt",
    )(pc, pc, pc, conv_w, pc, alog_row, dt_row)
    q, k, v, gb, sq, sk, sv = outs
    conv_state = jnp.concatenate([sq, sk, sv], axis=-1)[:, SUBLANES - (GDN_CONV - 1):, :]
    return q, k, v, gb, conv_state


def _dot_f32(a, b):
    return jnp.dot(a, b, precision=lax.Precision.HIGHEST, preferred_element_type=F32)


def _gdn_chunk_kernel(q_ref, k_ref, v_ref, gb_ref, z_ref, nw_ref, o_ref, sfin_ref, s_ref, *, chunks):
    c = GDN_CHUNK

    @pl.when(pl.program_id(2) == 0)
    def _():
        s_ref[...] = jnp.zeros_like(s_ref)

    r = lax.broadcasted_iota(jnp.int32, (c, c), 0)
    col = lax.broadcasted_iota(jnp.int32, (c, c), 1)
    lower = r >= col
    strict = r > col
    eye = jnp.where(r == col, 1.0, 0.0)
    r2 = lax.broadcasted_iota(jnp.int32, (2 * c, c), 0)
    c2 = lax.broadcasted_iota(jnp.int32, (2 * c, c), 1)
    pre_post = jnp.where(((r2 < c) & (c2 <= r2)) | ((r2 >= c) & (c2 > r2 - c)), 1.0, 0.0).astype(BF16)
    low01 = jnp.where(lower, 1.0, 0.0).astype(BF16)

    for ci in range(chunks):
        rows = slice(ci * c, (ci + 1) * c)
        q = q_ref[rows, :]
        k = k_ref[rows, :]
        v = v_ref[rows, :]
        g = gb_ref[rows, 0:1]
        beta = gb_ref[rows, 1:2]
        sums = _dot_exact01(pre_post, jnp.broadcast_to(g, (c, HEAD_DIM)))
        gc = sums[0:c, :]
        rest = sums[c:2 * c, :]
        diff = _dot_exact01(low01, jnp.where(strict, jnp.broadcast_to(g, (c, c)), 0.0))
        decay = jnp.where(lower, jnp.exp(jnp.where(lower, diff, 0.0)), 0.0)
        kb = k * beta
        lmat = jnp.where(strict, _dot_nt(kb.astype(BF16), k.astype(BF16)) * decay, 0.0)
        p = -lmat
        tinv = eye + p
        for _ in range(5):
            p = _dot_f32(p, p)
            tinv = tinv + _dot_f32(tinv, p)
        tb = tinv.astype(BF16)
        u = _dot(tb, (v * beta).astype(BF16))
        w = _dot(tb, (kb * jnp.exp(gc)).astype(BF16))
        aqk = jnp.where(lower, _dot_nt(q.astype(BF16), k.astype(BF16)) * decay, 0.0)
        q_dec = q * jnp.exp(gc)
        k_dec = k * jnp.exp(rest)
        s = s_ref[...]
        sb = s.astype(BF16)
        v_new = u - _dot(w.astype(BF16), sb)
        o = _dot(q_dec.astype(BF16), sb) + _dot(aqk.astype(BF16), v_new.astype(BF16))
        s_ref[...] = s * jnp.exp(gc[c - 1:c, :]) + _dot_tn(k_dec.astype(BF16), v_new.astype(BF16))
        zg = z_ref[rows, :]
        o_ref[rows, :] = _rms(o, nw_ref[...]) * _silu(zg)

    @pl.when(pl.program_id(2) == pl.num_programs(2) - 1)
    def _():
        sfin_ref[...] = s_ref[...]


def _gdn_chunked(q, k, v, gb, pc, norm_w, nseq, t, tm=256):
    nt = t // tm
    zb = C_GDN_Z // HEAD_DIM
    head_spec = pl.BlockSpec((tm, HEAD_DIM), lambda b, h, i: (b * nt + i, h))
    return pl.pallas_call(
        functools.partial(_gdn_chunk_kernel, chunks=tm // GDN_CHUNK),
        grid=(nseq, N_HEADS, nt),
        in_specs=[head_spec, head_spec, head_spec, head_spec,
                  pl.BlockSpec((tm, HEAD_DIM), lambda b, h, i: (b * nt + i, zb + h)),
                  pl.BlockSpec((1, HEAD_DIM), lambda b, h, i: (0, 0))],
        out_specs=[head_spec, pl.BlockSpec((None, None, HEAD_DIM, HEAD_DIM), lambda b, h, i: (b, h, 0, 0))],
        out_shape=[jax.ShapeDtypeStruct((nseq * t, GDN_W), F32),
                   jax.ShapeDtypeStruct((nseq, N_HEADS, HEAD_DIM, HEAD_DIM), F32)],
        scratch_shapes=[pltpu.VMEM((HEAD_DIM, HEAD_DIM), F32)],
        compiler_params=_cparams("parallel", "parallel", "arbitrary"),
        name="gdn_chunked",
    )(q, k, v, gb, pc, norm_w.reshape(1, HEAD_DIM))


def _gdn_prep_sample_kernel(*refs):
    u_refs = refs[0:3]
    p_refs = refs[3:12]
    cw_ref, small_ref, alog_ref, dt_ref, bfox_ref = refs[12:17]
    q_ref, k_ref, v_ref, gb_ref, lf_ref = refs[17:22]

    def conv(idx):
        cw = cw_ref[:, idx * GDN_W:(idx + 1) * GDN_W]
        out = cw[GDN_CONV - 1:GDN_CONV, :] * u_refs[idx][...]
        for r in range(GDN_CONV - 1):
            out = out + cw[r:r + 1, :] * p_refs[3 * r + idx][...]
        return _silu(out)

    q_ref[...] = _l2norm_heads(conv(0), HEAD_DIM ** -0.5)
    k_ref[...] = _l2norm_heads(conv(1), 1.0)
    v_ref[...] = conv(2)
    gb_ref[...] = _gdn_gates(small_ref[...], alog_ref[...], dt_ref[...])
    lf_ref[...] = _log_sigmoid(small_ref[...] + bfox_ref[...])


def _gdn_prep_sample(pc, prev, conv_w, alog_row, dt_row, bfox_row):
    m = pc.shape[0]
    ub = C_GDN_QKV // GDN_W
    prev2 = prev.reshape(m, (GDN_CONV - 1) * 3 * GDN_W)
    row_spec = pl.BlockSpec((m, GDN_W), lambda i: (0, 0))
    par_spec = pl.BlockSpec((1, LANES), lambda i: (0, 0))
    in_specs = [pl.BlockSpec((m, GDN_W), functools.partial(lambda i, c: (0, c), c=ub + n)) for n in range(3)]
    in_specs += [pl.BlockSpec((m, GDN_W), functools.partial(lambda i, c: (0, c), c=n)) for n in range(9)]
    in_specs += [pl.BlockSpec((GDN_CONV, 3 * GDN_W), lambda i: (0, 0)),
                 pl.BlockSpec((m, LANES), lambda i: (0, C_SMALL // LANES)), par_spec, par_spec, par_spec]
    return pl.pallas_call(
        _gdn_prep_sample_kernel,
        grid=(1,),
        in_specs=in_specs,
        out_specs=[row_spec] * 4 + [pl.BlockSpec((m, LANES), lambda i: (0, 0))],
        out_shape=[jax.ShapeDtypeStruct((m, GDN_W), F32)] * 4 + [jax.ShapeDtypeStruct((m, LANES), F32)],
        compiler_params=_cparams("arbitrary"),
        name="gdn_prep_sample",
    )(pc, pc, pc, *([prev2] * 9), conv_w, pc, alog_row, dt_row, bfox_row)


def _gdn_step_kernel(q_ref, k_ref, v_ref, gb_ref, z_ref, nw_ref, s_ref, o_ref, so_ref, *, tb):
    pad = jnp.zeros((HEAD_DIM - tb, HEAD_DIM), F32)
    for h in range(N_HEADS):
        cols = slice(h * HEAD_DIM, (h + 1) * HEAD_DIM)
        k_t = jnp.concatenate([k_ref[:, cols], pad], axis=0).T
        q_t = jnp.concatenate([q_ref[:, cols], pad], axis=0).T
        for b in range(tb):
            decay = jnp.exp(gb_ref[b:b + 1, h * HEAD_DIM:h * HEAD_DIM + 1])
            beta = gb_ref[b:b + 1, h * HEAD_DIM + 1:h * HEAD_DIM + 2]
            kc = k_t[:, b:b + 1]
            s = s_ref[b, h] * decay
            v_old = jnp.sum(s * kc, axis=0, keepdims=True)
            s = s + kc * ((v_ref[b:b + 1, cols] - v_old) * beta)
            so_ref[b, h] = s
            o = jnp.sum(s * q_t[:, b:b + 1], axis=0, keepdims=True)
            o_ref[b:b + 1, cols] = _rms(o, nw_ref[...]) * _silu(z_ref[b:b + 1, cols])


def _gdn_step(q, k, v, gb, pc, norm_w, state, tb=8):
    m = q.shape[0]
    row_spec = pl.BlockSpec((tb, GDN_W), lambda i: (i, 0))
    st_spec = pl.BlockSpec((tb, N_HEADS, HEAD_DIM, HEAD_DIM), lambda i: (i, 0, 0, 0))
    return pl.pallas_call(
        functools.partial(_gdn_step_kernel, tb=tb),
        grid=(m // tb,),
        in_specs=[row_spec, row_spec, row_spec, row_spec,
                  pl.BlockSpec((tb, GDN_W), lambda i: (i, C_GDN_Z // GDN_W)),
                  pl.BlockSpec((1, HEAD_DIM), lambda i: (0, 0)), st_spec],
        out_specs=[row_spec, st_spec],
        out_shape=[jax.ShapeDtypeStruct((m, GDN_W), F32), jax.ShapeDtypeStruct(state.shape, F32)],
        compiler_params=_cparams("parallel"),
        name="gdn_step",
    )(q, k, v, gb, pc, norm_w.reshape(1, HEAD_DIM), state)


def _dec_idx_kernel(pt_ref, q_ref, w_ref, knew_ref, *refs, page):
    kidx_refs, (sc_ref, self_ref) = refs[:-2], refs[-2:]
    q = q_ref[...]
    qb = q.astype(BF16)
    w = w_ref[...] * (IDX_DIM ** -0.5 * IDX_HEADS ** -0.5)
    for pg, kidx_ref in enumerate(kidx_refs):
        si = _dot_nt(qb, kidx_ref[...].astype(BF16))
        sc_ref[:, pg * page:(pg + 1) * page] = jnp.sum(jnp.maximum(si, 0.0) * w, axis=0, keepdims=True)
    si_new = jnp.sum(q * knew_ref[...], axis=-1, keepdims=True)
    own = jnp.sum(jnp.maximum(si_new, 0.0) * w, axis=0, keepdims=True)
    self_ref[...] = jnp.broadcast_to(own, self_ref.shape)


def _dec_idx_scores(layer, page_table, qidx3, w3, knew3, cache_kidx):
    nb, npages = page_table.shape
    page = cache_kidx.shape[2]
    page_specs = [pl.BlockSpec((None, None, page, IDX_DIM),
                               functools.partial(lambda b, pt, pg: (layer, pt[b, pg], 0, 0), pg=pg))
                  for pg in range(npages)]
    grid_spec = pltpu.PrefetchScalarGridSpec(
        num_scalar_prefetch=1,
        grid=(nb,),
        in_specs=[pl.BlockSpec((None, IDX_HEADS, IDX_DIM), lambda b, pt: (b, 0, 0)),
                  pl.BlockSpec((None, IDX_HEADS, 1), lambda b, pt: (b, 0, 0)),
                  pl.BlockSpec((None, 1, IDX_DIM), lambda b, pt: (b, 0, 0))] + page_specs,
        out_specs=[pl.BlockSpec((None, 1, npages * page), lambda b, pt: (b, 0, 0)),
                   pl.BlockSpec((None, 1, LANES), lambda b, pt: (b, 0, 0))])
    return pl.pallas_call(
        functools.partial(_dec_idx_kernel, page=page),
        grid_spec=grid_spec,
        out_shape=[jax.ShapeDtypeStruct((nb, 1, npages * page), F32), jax.ShapeDtypeStruct((nb, 1, LANES), F32)],
        compiler_params=_cparams("parallel"),
        name="dec_idx_scores",
    )(page_table, qidx3, w3, knew3, *([cache_kidx] * npages))


def _dec_thr_kernel(sc_ref, self_ref, sel_ref, own_sel_ref, *, topk):
    scores = sc_ref[...]
    keys = _float_key(scores)
    own = _float_key(self_ref[:, 0:1])

    def count_ge(cand):
        cnt = jnp.sum(jnp.where(keys >= cand, 1, 0), axis=-1, keepdims=True)
        return cnt + jnp.where(own >= cand, 1, 0)

    res = jnp.maximum(_kth_largest_key(count_ge, keys.shape[0], topk), KEY_NEG_INF)
    thr = _key_float(res)
    own_sel_ref[...] = jnp.broadcast_to(jnp.where(self_ref[:, 0:1] >= thr, 1.0, 0.0), own_sel_ref.shape)
    r = lax.broadcasted_iota(jnp.int32, (LANES, N_KV * LANES), 0)
    c = lax.broadcasted_iota(jnp.int32, (LANES, N_KV * LANES), 1)
    spread = jnp.where(r == c // N_KV, 1.0, 0.0).astype(BF16)
    for j in range(scores.shape[1] // LANES):
        sel = jnp.where(scores[:, j * LANES:(j + 1) * LANES] >= thr, 1.0, 0.0).astype(BF16)
        sel_ref[:, j * N_KV * LANES:(j + 1) * N_KV * LANES] = _dot(sel, spread)


def _dec_threshold(scores, own, topk):
    nb, n = scores.shape
    return pl.pallas_call(
        functools.partial(_dec_thr_kernel, topk=topk),
        grid=(1,),
        in_specs=[pl.BlockSpec((nb, n), lambda i: (0, 0)), pl.BlockSpec((nb, LANES), lambda i: (0, 0))],
        out_specs=[pl.BlockSpec((nb, N_KV * n), lambda i: (0, 0)), pl.BlockSpec((nb, LANES), lambda i: (0, 0))],
        out_shape=[jax.ShapeDtypeStruct((nb, N_KV * n), F32), jax.ShapeDtypeStruct((nb, LANES), F32)],
        compiler_params=_cparams("arbitrary"),
        name="dec_threshold",
    )(scores, own)


N_CACHES = 6


def _dec_attn_kernel(pt_ref, fq_ref, sq_ref, dq_ref, fkn_ref, fvn_ref, dkn_ref, dvn_ref, lfn_ref, sel_ref, own_ref,
                     bias0_ref, bias_ref, fk_hbm, fv_hbm, sk_hbm, sv_hbm, dk_hbm, dv_hbm, lf_hbm,
                     of_ref, os_ref, od_ref, kv_buf, lf_buf, sems, *, layer, npages, page):
    b = pl.program_id(0)
    nb = pl.num_programs(0)
    rows = N_KV * page
    n = npages * rows
    caches = (fk_hbm, fv_hbm, sk_hbm, sv_hbm, dk_hbm, dv_hbm)
    scale = HEAD_DIM ** -0.5

    def page_copies(seq, slot):
        out = []
        for pg in range(npages):
            pid = pt_ref[seq, pg]
            for c, hbm in enumerate(caches):
                out.append(pltpu.make_async_copy(hbm.at[layer, pid], kv_buf.at[slot, c, pl.ds(pg * rows, rows)],
                                                 sems.at[slot, c]))
            out.append(pltpu.make_async_copy(lf_hbm.at[layer, pid], lf_buf.at[slot, pg], sems.at[slot, N_CACHES]))
        return out

    slot = b % 2

    @pl.when(b == 0)
    def _():
        for cp in page_copies(0, 0):
            cp.start()

    @pl.when(b + 1 < nb)
    def _():
        for cp in page_copies(b + 1, 1 - slot):
            cp.start()

    for cp in page_copies(b, slot):
        cp.wait()

    row = lax.broadcasted_iota(jnp.int32, (SUBLANES, n), 0)
    col = lax.broadcasted_iota(jnp.int32, (SUBLANES, n), 1)
    mine = (col % N_KV) == jnp.where(row < 2, 0, 1)
    top = lax.broadcasted_iota(jnp.int32, (SUBLANES, HEAD_DIM), 0) < 2

    def per_head(x2):
        return jnp.where(top, x2[0:1, :], x2[1:2, :])

    def slab(c):
        return kv_buf[slot, c].astype(BF16)

    rr = lax.broadcasted_iota(jnp.int32, (page, rows), 0)
    cc = lax.broadcasted_iota(jnp.int32, (page, rows), 1)
    pos_after = jnp.where(rr > cc // N_KV, 1.0, 0.0).astype(BF16)
    carry = lfn_ref[:, 0:1]
    decay = [None] * npages
    for pg in reversed(range(npages)):
        lf = lf_buf[slot, pg]
        hi, mid, lo = _split3(lf)
        decay[pg] = _dot(hi, pos_after) + _dot(mid, pos_after) + _dot(lo, pos_after) + carry
        carry = carry + jnp.sum(lf, axis=-1, keepdims=True)
    s = _dot_nt(fq_ref[...].astype(BF16), slab(0)) * scale + jnp.concatenate(decay, axis=-1)
    s = jnp.where(mine, s, MASKED)
    s_own = jnp.sum(fq_ref[...] * per_head(fkn_ref[...]), axis=-1, keepdims=True) * scale
    m = jnp.maximum(jnp.max(s, axis=-1, keepdims=True), s_own)
    w = jnp.where(mine, jnp.exp(s - m), 0.0)
    w_own = jnp.exp(s_own - m)
    acc = _dot(w.astype(BF16), slab(1)) + w_own * per_head(fvn_ref[...])
    of_ref[...] = acc / (jnp.sum(w, axis=-1, keepdims=True) + w_own)

    z = _dot_nt(sq_ref[...].astype(BF16), slab(2)) * scale
    sp = jnp.log(1.0 + jnp.exp(-jnp.abs(z)))
    log_take = jnp.minimum(z, 0.0) - sp
    log_stay = jnp.where(mine, jnp.minimum(-z, 0.0) - sp, 0.0)
    r2 = lax.broadcasted_iota(jnp.int32, (rows, rows), 0)
    c2 = lax.broadcasted_iota(jnp.int32, (rows, rows), 1)
    row_after = jnp.where(r2 // N_KV > c2 // N_KV, 1.0, 0.0).astype(BF16)
    carry = jnp.zeros((SUBLANES, 1), F32)
    later = [None] * npages
    for pg in reversed(range(npages)):
        ls = log_stay[:, pg * rows:(pg + 1) * rows]
        hi, lo = _split2(ls)
        later[pg] = _dot(hi, row_after) + _dot(lo, row_after) + carry
        carry = carry + jnp.sum(ls, axis=-1, keepdims=True)
    a = jnp.where(mine, jnp.exp(log_take + jnp.concatenate(later, axis=-1)), 0.0)
    os_ref[...] = _dot(a.astype(BF16), slab(3))

    keep = mine & (sel_ref[...] > 0.5)
    s = jnp.where(keep, _dot_nt(dq_ref[...].astype(BF16), slab(4)) * scale + bias_ref[...], MASKED)
    own_on = own_ref[:, 0:1] > 0.5
    d_own = jnp.sum(dq_ref[...] * per_head(dkn_ref[...]), axis=-1, keepdims=True) * scale + bias0_ref[:, 0:1]
    d_own = jnp.where(own_on, d_own, MASKED)
    m = jnp.maximum(jnp.max(s, axis=-1, keepdims=True), d_own)
    w = jnp.where(keep, jnp.exp(s - m), 0.0)
    w_own = jnp.where(own_on, jnp.exp(d_own - m), 0.0)
    acc = _dot(w.astype(BF16), slab(5)) + w_own * per_head(dvn_ref[...])
    od_ref[...] = acc / (jnp.sum(w, axis=-1, keepdims=True) + w_own)


def _dec_attention(layer, page_table, q8, new_kv, lfn8, sel3, own3, bias0, bias_dec, caches, lf_cache8):
    nb, npages = page_table.shape
    page = caches[0].shape[2] // N_KV
    n = npages * page * N_KV
    per_b8 = pl.BlockSpec((None, SUBLANES, HEAD_DIM), lambda b, pt: (b, 0, 0))
    per_b2 = pl.BlockSpec((None, N_KV, HEAD_DIM), lambda b, pt: (b, 0, 0))
    hbm = pl.BlockSpec(memory_space=pl.ANY)
    in_specs = [per_b8, per_b8, per_b8, per_b2, per_b2, per_b2, per_b2, per_b8,
                pl.BlockSpec((None, 1, n), lambda b, pt: (b, 0, 0)),
                pl.BlockSpec((None, 1, LANES), lambda b, pt: (b, 0, 0)),
                pl.BlockSpec((SUBLANES, LANES), lambda b, pt: (0, 0)),
                pl.BlockSpec((SUBLANES, n), lambda b, pt: (0, 0))] + [hbm] * (N_CACHES + 1)
    grid_spec = pltpu.PrefetchScalarGridSpec(
        num_scalar_prefetch=1, grid=(nb,), in_specs=in_specs,
        out_specs=[per_b8, per_b8, per_b8],
        scratch_shapes=[pltpu.VMEM((2, N_CACHES, n, HEAD_DIM), F32),
                        pltpu.VMEM((2, npages, SUBLANES, page), F32),
                        pltpu.SemaphoreType.DMA((2, N_CACHES + 1))])
    return pl.pallas_call(
        functools.partial(_dec_attn_kernel, layer=layer, npages=npages, page=page),
        grid_spec=grid_spec,
        out_shape=[jax.ShapeDtypeStruct((nb, SUBLANES, HEAD_DIM), F32)] * 3,
        compiler_params=_cparams("arbitrary"),
        name="dec_attention",
    )(page_table, *q8, *new_kv, lfn8, sel3, own3, bias0, bias_dec, *caches, lf_cache8)


def _lane_row(values, offset):
    return jnp.zeros((1, LANES), F32).at[0, offset:offset + values.shape[0]].set(values.astype(F32))


def _layer_params(l, w_mod, b_mod, g_mix_pre, g_mix_post, g_ffn_pre, g_ffn_post, w_in, b_fox_f, gdn_a_log,
                  gdn_dt_bias, gdn_conv_w, gdn_norm_w, w_out, w_up, ffn_conv_w, ffn_conv_b, w_down):
    perm, n_real = _in_proj_permutation(w_in.shape[1])
    w_in_p = jnp.pad(w_in[l][:, perm], ((0, 0), (0, D_IN_PAD - n_real))).astype(BF16)
    return dict(
        w_mod=w_mod[l], b_mod=b_mod[l], g_mix_pre=g_mix_pre[l], g_mix_post=g_mix_post[l],
        g_ffn_pre=g_ffn_pre[l], g_ffn_post=g_ffn_post[l], w_in=w_in_p,
        b_fox_row=_lane_row(b_fox_f[l], S_FOX_F), alog_row=_lane_row(gdn_a_log[l], S_GDN_A),
        dt_row=_lane_row(gdn_dt_bias[l], S_GDN_A), gdn_conv_w=gdn_conv_w[l], gdn_norm_w=gdn_norm_w[l],
        w_out=w_out[l].astype(BF16), w_up=w_up[l].astype(BF16), ffn_conv_w=ffn_conv_w[l],
        ffn_conv_b=ffn_conv_b[l], w_down=w_down[l].astype(BF16))


def _kv_state(pc, col, lead):
    return pc[:, col:col + N_KV * HEAD_DIM].reshape(*lead, N_KV, HEAD_DIM)


def _prompt_layer(x, mod, p, rel_tiles, nseq, t):
    pc = _in_proj(x, p['g_mix_pre'], mod, p['w_in'], t, tm=1024, tn=512)
    lf, cum = _fox_gates(pc, p['b_fox_row'], nseq, t)
    cum4 = cum[:, :N_HEADS]
    cum_cols = jnp.pad(cum4.reshape(nseq * t, N_KV, 2), ((0, 0), (0, 0), (0, LANES - 2)))
    cum_cols = cum_cols.reshape(nseq * t, N_KV * LANES)
    cum_rows = jnp.transpose(cum4.reshape(nseq, t, N_KV, 2), (0, 2, 3, 1)).reshape(nseq * N_KV, 2, t)
    o_fox = _flash(pc, nseq, t, 'fox', (cum_cols, cum_rows))
    o_sb = _sb_attention(pc, nseq, t)
    gq, gk, gv, gb, gdn_conv = _gdn_prep_prompt(pc, p['gdn_conv_w'], p['alog_row'], p['dt_row'], nseq, t)
    o_gdn, gdn_s = _gdn_chunked(gq, gk, gv, gb, pc, p['gdn_norm_w'], nseq, t)
    scores, thr = _dsa_scores(pc, nseq, t, min(TOPK_MAX, t // 4))
    o_dsa = _flash(pc, nseq, t, 'dsa', (rel_tiles, scores, thr))
    x = _out_proj((o_fox, o_sb, o_gdn, o_dsa), p['w_out'], x, p['g_mix_post'], mod, t, tm=256)
    x, ffn_conv = _ffn_prompt(x, p['g_ffn_pre'], mod, p['w_up'], p['ffn_conv_w'], p['ffn_conv_b'], p['w_down'],
                              p['g_ffn_post'], t, tm=512, tf=512)
    lead = (nseq, t)
    states = (_kv_state(pc, C_FOX_K, lead), _kv_state(pc, C_FOX_V, lead), lf[:, :N_HEADS].reshape(nseq, t, N_HEADS),
              _kv_state(pc, C_SB_K, lead), _kv_state(pc, C_SB_V, lead),
              _kv_state(pc, C_DSA_K, lead), _kv_state(pc, C_DSA_V, lead),
              pc[:, C_DSA_KIDX:C_DSA_KIDX + IDX_DIM].reshape(nseq, t, IDX_DIM), gdn_s, gdn_conv, ffn_conv)
    return x, states


def _heads8(x):
    nb = x.shape[0]
    return jnp.pad(x.reshape(nb, N_HEADS, HEAD_DIM), ((0, 0), (0, SUBLANES - N_HEADS), (0, 0)))


def _sample_layer(layer, x, mod, p, caches, lf_cache8, kidx_cache, page_table, bias_dec, bias0,
                  gdn_s, gdn_conv, ffn_conv):
    nb = x.shape[0]
    pc = _in_proj(x, p['g_mix_pre'], mod, p['w_in'], 1, tm=nb, tn=512)

    def cols(c, n):
        return pc[:, c:c + n]

    gq, gk, gv, gb, lf = _gdn_prep_sample(pc, gdn_conv, p['gdn_conv_w'], p['alog_row'], p['dt_row'], p['b_fox_row'])
    o_gdn, gdn_s_new = _gdn_step(gq, gk, gv, gb, pc, p['gdn_norm_w'], gdn_s)
    gdn_conv_new = jnp.concatenate([gdn_conv[:, 1:], cols(C_GDN_QKV, 3 * GDN_W)[:, None]], axis=1)

    qidx3 = cols(C_DSA_QIDX, IDX_HEADS * IDX_DIM).reshape(nb, IDX_HEADS, IDX_DIM)
    w3 = cols(C_SMALL + S_DSA_W, IDX_HEADS).reshape(nb, IDX_HEADS, 1)
    knew3 = cols(C_DSA_KIDX, IDX_DIM).reshape(nb, 1, IDX_DIM)
    scores3, own3 = _dec_idx_scores(layer, page_table, qidx3, w3, knew3, kidx_cache)
    past = scores3.shape[-1]
    sel, own_sel = _dec_threshold(scores3.reshape(nb, past), own3.reshape(nb, LANES),
                                  min(TOPK_MAX, (past + 1) // 4))

    lf4 = lf[:, :N_HEADS]
    lfn8 = jnp.broadcast_to(jnp.pad(lf4, ((0, 0), (0, SUBLANES - N_HEADS)))[:, :, None], (nb, SUBLANES, LANES))
    q8 = (_heads8(cols(C_FOX_Q, 512)), _heads8(cols(C_SB_Q, 512)), _heads8(cols(C_DSA_Q, 512)))
    new_kv = tuple(cols(c, 256).reshape(nb, N_KV, HEAD_DIM) for c in (C_FOX_K, C_FOX_V, C_DSA_K, C_DSA_V))
    o8 = _dec_attention(layer, page_table, q8, new_kv, lfn8, sel.reshape(nb, 1, N_KV * past),
                        own_sel.reshape(nb, 1, LANES), bias0, bias_dec, caches, lf_cache8)
    o_fox, o_sb, o_dsa = (o[:, :N_HEADS].reshape(nb, N_HEADS * HEAD_DIM) for o in o8)

    x = _out_proj((o_fox, o_sb, o_gdn, o_dsa), p['w_out'], x, p['g_mix_post'], mod, 1, tm=nb)
    x, up = _ffn_sample(x, p['g_ffn_pre'], mod, p['w_up'], p['ffn_conv_w'], p['ffn_conv_b'], p['w_down'],
                        p['g_ffn_post'], ffn_conv, tf=512)
    ffn_conv_new = jnp.concatenate([ffn_conv[:, 1:], up[:, None]], axis=1)
    lead = (nb, 1)
    states = (_kv_state(pc, C_FOX_K, lead), _kv_state(pc, C_FOX_V, lead), lf4.reshape(nb, 1, N_HEADS),
              _kv_state(pc, C_SB_K, lead), _kv_state(pc, C_SB_V, lead),
              _kv_state(pc, C_DSA_K, lead), _kv_state(pc, C_DSA_V, lead),
              cols(C_DSA_KIDX, IDX_DIM).reshape(nb, 1, IDX_DIM), gdn_s_new, gdn_conv_new, ffn_conv_new)
    return x, states


def kernel(x_prompt, x_sample, cache_fox_k, cache_fox_v, cache_fox_logf, cache_sb_k, cache_sb_v, cache_dsa_k,
           cache_dsa_v, cache_dsa_kidx, state_gdn_s, state_gdn_conv, state_ffn_conv, page_table, c_prompt,
           c_sample, w_mod, b_mod, g_mix_pre, g_mix_post, g_ffn_pre, g_ffn_post, w_in, b_fox_f, gdn_a_log,
           gdn_dt_bias, gdn_conv_w, gdn_norm_w, rel_bias, w_out, w_up, ffn_conv_w, ffn_conv_b, w_down):
    nseq, t, d = x_prompt.shape
    nsamp = x_sample.shape[0]
    depth = w_in.shape[0]
    xp = x_prompt.reshape(nseq * t, d)
    xs = x_sample.reshape(nsamp, d)
    pad = (-(nseq + nsamp)) % SUBLANES
    c_all = jnp.concatenate([c_prompt, c_sample, jnp.zeros((pad, d), F32)], axis=0)
    rel_tiles = _rel_bias_tiles(rel_bias, 256)

    pool_shape = cache_fox_k.shape[:3]
    caches = tuple(c.reshape(pool_shape[0], pool_shape[1], pool_shape[2] * N_KV, HEAD_DIM)
                   for c in (cache_fox_k, cache_fox_v, cache_sb_k, cache_sb_v, cache_dsa_k, cache_dsa_v))
    lf_cache8 = jnp.pad(jnp.swapaxes(cache_fox_logf, 2, 3), ((0, 0), (0, 0), (0, SUBLANES - N_HEADS), (0, 0)))
    past = page_table.shape[1] * pool_shape[2]
    rel_dec = past - jnp.arange(N_KV * past, dtype=jnp.int32) // N_KV
    bias_dec = jnp.pad(rel_bias[_t5_bucket(rel_dec)].astype(F32).T, ((0, SUBLANES - N_HEADS), (0, 0)))
    bias0 = jnp.broadcast_to(jnp.pad(rel_bias[0].astype(F32), (0, SUBLANES - N_HEADS))[:, None], (SUBLANES, LANES))

    st_prompt, st_sample = [], []
    for l in range(depth):
        p = _layer_params(l, w_mod, b_mod, g_mix_pre, g_mix_post, g_ffn_pre, g_ffn_post, w_in, b_fox_f, gdn_a_log,
                          gdn_dt_bias, gdn_conv_w, gdn_norm_w, w_out, w_up, ffn_conv_w, ffn_conv_b, w_down)
        mod = _modulation(c_all, p['w_mod'], p['b_mod'])
        xp, sp = _prompt_layer(xp, mod[:nseq], p, rel_tiles, nseq, t)
        xs, ss = _sample_layer(l, xs, mod[nseq:nseq + nsamp], p, caches, lf_cache8, cache_dsa_kidx, page_table,
                               bias_dec, bias0, state_gdn_s[l], state_gdn_conv[l], state_ffn_conv[l])
        st_prompt.append(sp)
        st_sample.append(ss)
    sp = [jnp.stack(z) for z in zip(*st_prompt)]
    ss = [jnp.stack(z) for z in zip(*st_sample)]
    out = [xp.reshape(nseq, t, d), xs.reshape(nsamp, 1, d)]
    for a, b in zip(sp, ss):
        out += [a, b]
    return tuple(out)
```

```python
import functools
import math

import numpy as np
import jax
import jax.numpy as jnp
from jax import lax
from jax.experimental import pallas as pl
from jax.experimental.pallas import tpu as pltpu

F32 = jnp.float32
BF16 = jnp.bfloat16

HEAD_DIM = 128
N_HEADS = 4
N_KV = 2
IDX_HEADS = 16
IDX_DIM = 128
GDN_CONV = 4
GDN_CHUNK = 64
FFN_CONV = 3
TOPK_MAX = 256
N_BUCKETS = 32
MAX_DISTANCE = 128
EPS = 1e-6
NEG_INF = float("-inf")

LANES = 128
SUBLANES = 8
VMEM_LIMIT = 56 * 1024 * 1024
ATTN_TILE = 512

C_FOX_Q, C_FOX_K, C_FOX_V = 0, 512, 768
C_SB_Q, C_SB_K, C_SB_V = 1024, 1536, 1792
C_GDN_QKV, C_GDN_Z = 2048, 3584
C_DSA_QIDX = 4096
C_DSA_Q, C_DSA_K, C_DSA_V = 6144, 6656, 6912
C_DSA_KIDX, C_SMALL = 7168, 7296
D_IN_PAD = 7680
S_FOX_F, S_GDN_A, S_GDN_B, S_DSA_W = 0, 4, 8, 12


def _in_proj_permutation(d_model):
    gdn_qkv = N_HEADS * 3 * HEAD_DIM
    layout = (
        ('fox_q', 512), ('fox_k', 256), ('fox_v', 256), ('fox_f', 4),
        ('sb_q', 512), ('sb_k', 256), ('sb_v', 256),
        ('gdn_qkv', gdn_qkv), ('gdn_a', 4), ('gdn_b', 4), ('gdn_z', 512),
        ('dsa_q', 512), ('dsa_k', 256), ('dsa_v', 256),
        ('dsa_qidx', IDX_HEADS * IDX_DIM), ('dsa_kidx', IDX_DIM), ('dsa_w', IDX_HEADS),
    )
    off = {}
    o = 0
    for name, size in layout:
        off[name] = (o, size)
        o += size
    order = ('fox_q', 'fox_k', 'fox_v', 'sb_q', 'sb_k', 'sb_v', 'gdn_qkv', 'gdn_z', 'dsa_qidx', 'dsa_q', 'dsa_k',
             'dsa_v', 'dsa_kidx', 'fox_f', 'gdn_a', 'gdn_b', 'dsa_w')
    return [off[n] for n in order], o


def _cparams(*sem):
    return pltpu.CompilerParams(dimension_semantics=sem, vmem_limit_bytes=VMEM_LIMIT)


def _rms(x, g):
    return x * lax.rsqrt(jnp.mean(x * x, axis=-1, keepdims=True) + EPS) * g


def _silu(x):
    return x * (1.0 / (1.0 + jnp.exp(-x)))


def _log_sigmoid(x):
    return jnp.minimum(x, 0.0) - jnp.log(1.0 + jnp.exp(-jnp.abs(x)))


def _split3(x):
    hi = x.astype(BF16)
    r1 = x - hi.astype(F32)
    mid = r1.astype(BF16)
    lo = (r1 - mid.astype(F32)).astype(BF16)
    return hi, mid, lo


def _dot(a, b):
    return jnp.dot(a, b, preferred_element_type=F32)


def _dot_nt(a, b):
    return lax.dot_general(a, b, (((1,), (1,)), ((), ())), preferred_element_type=F32)


def _dot_tn(a, b):
    return lax.dot_general(a, b, (((0,), (0,)), ((), ())), preferred_element_type=F32)


def _mod_kernel(c_ref, w_ref, b_ref, o_ref):
    a = _silu(c_ref[...]).astype(BF16)
    o_ref[...] = _dot(a, w_ref[...].astype(BF16)) + b_ref[...]


def _modulation(c, w_mod, b_mod):
    m, d = c.shape
    n = w_mod.shape[1]
    tn = 1024
    return pl.pallas_call(
        _mod_kernel,
        grid=(n // tn,),
        in_specs=[pl.BlockSpec((m, d), lambda j: (0, 0)),
                  pl.BlockSpec((d, tn), lambda j: (0, j)),
                  pl.BlockSpec((1, tn), lambda j: (0, j))],
        out_specs=pl.BlockSpec((m, tn), lambda j: (0, j)),
        out_shape=jax.ShapeDtypeStruct((m, n), F32),
        compiler_params=_cparams("parallel"),
        name="modulation",
    )(c, w_mod, b_mod.reshape(1, n))


def _mod_spec(mod, chunk, tm, rows_per_seq, d):
    if rows_per_seq == 1:
        return mod, pl.BlockSpec((tm, d), lambda i, *_: (i, chunk))
    tiles_per_seq = rows_per_seq // tm
    return (mod.reshape(mod.shape[0], 1, mod.shape[1]),
            pl.BlockSpec((None, 1, d), lambda i, *_: (i // tiles_per_seq, 0, chunk)))


def _in_proj_kernel(x_ref, g_ref, sc_ref, sh_ref, w_ref, o_ref, h_ref):
    @pl.when(pl.program_id(1) == 0)
    def _():
        h = _rms(x_ref[...], g_ref[...]) * (1.0 + sc_ref[...]) + sh_ref[...]
        h_ref[...] = h.astype(BF16)

    o_ref[...] = _dot(h_ref[...], w_ref[...])


def _in_proj(x, g, mod, w, rows_per_seq, tm, tn):
    m, d = x.shape
    n = w.shape[1]
    sc, sc_spec = _mod_spec(mod, 1, tm, rows_per_seq, d)
    sh, sh_spec = _mod_spec(mod, 0, tm, rows_per_seq, d)
    return pl.pallas_call(
        _in_proj_kernel,
        grid=(m // tm, n // tn),
        in_specs=[pl.BlockSpec((tm, d), lambda i, j: (i, 0)),
                  pl.BlockSpec((1, d), lambda i, j: (0, 0)),
                  sc_spec, sh_spec,
                  pl.BlockSpec((d, tn), lambda i, j: (0, j))],
        out_specs=pl.BlockSpec((tm, tn), lambda i, j: (i, j)),
        out_shape=jax.ShapeDtypeStruct((m, n), F32),
        scratch_shapes=[pltpu.VMEM((tm, d), BF16)],
        compiler_params=_cparams("parallel", "arbitrary"),
        name="in_proj",
    )(x, g.reshape(1, d), sc, sh, w)


def _out_proj_kernel(a0_ref, a1_ref, a2_ref, a3_ref, w_ref, x_ref, g_ref, gate_ref, o_ref):
    m = None
    for n, a_ref in enumerate((a0_ref, a1_ref, a2_ref, a3_ref)):
        k = a_ref.shape[1]
        part = _dot(a_ref[...].astype(BF16), w_ref[n * k:(n + 1) * k, :])
        m = part if m is None else m + part
    o_ref[...] = x_ref[...] + gate_ref[...] * _rms(m, g_ref[...])


def _out_proj(mixed, w, x, g, mod, rows_per_seq, tm):
    m, k4 = mixed[0].shape
    k, d = w.shape
    gate, gate_spec = _mod_spec(mod, 2, tm, rows_per_seq, d)
    a_spec = pl.BlockSpec((tm, k4), lambda i: (i, 0))
    return pl.pallas_call(
        _out_proj_kernel,
        grid=(m // tm,),
        in_specs=[a_spec, a_spec, a_spec, a_spec,
                  pl.BlockSpec((k, d), lambda i: (0, 0)),
                  pl.BlockSpec((tm, d), lambda i: (i, 0)),
                  pl.BlockSpec((1, d), lambda i: (0, 0)),
                  gate_spec],
        out_specs=pl.BlockSpec((tm, d), lambda i: (i, 0)),
        out_shape=jax.ShapeDtypeStruct((m, d), F32),
        compiler_params=_cparams("parallel"),
        name="out_proj",
    )(*mixed, w, x, g.reshape(1, d), gate)


def _shift_rows(u, prev, k):
    rolled = pltpu.roll(u, k, axis=0)
    row = lax.broadcasted_iota(jnp.int32, u.shape, 0)
    out = rolled
    for r in range(k):
        out = jnp.where(row == r, prev[SUBLANES - k + r:SUBLANES - k + r + 1, :], out)
    return out


def _ffn_prompt_kernel(x_ref, g_ref, sc_ref, sh_ref, wg_ref, wv_ref, cwg_ref, cwv_ref, cbg_ref, cbv_ref,
                       wd_ref, gpost_ref, gate_ref, o_ref, sg_ref, sv_ref,
                       h_ref, acc_ref, cg_ref, cv_ref, *, tiles_per_seq):
    i = pl.program_id(0)
    j = pl.program_id(1)

    @pl.when(j == 0)
    def _():
        h = _rms(x_ref[...], g_ref[...]) * (1.0 + sc_ref[...]) + sh_ref[...]
        h_ref[...] = h.astype(BF16)
        acc_ref[...] = jnp.zeros_like(acc_ref)

    fresh = i % tiles_per_seq == 0

    def conv(u, carry_ref, cw_ref, cb_ref):
        prev = jnp.where(fresh, 0.0, carry_ref[j])
        out = cw_ref[0:1, :] * _shift_rows(u, prev, 2) + cw_ref[1:2, :] * _shift_rows(u, prev, 1)
        out = out + cw_ref[2:3, :] * u + cb_ref[...]
        carry_ref[j] = u[u.shape[0] - SUBLANES:, :]
        return out

    ug = _dot(h_ref[...], wg_ref[...])
    uv = _dot(h_ref[...], wv_ref[...])
    sg_ref[...] = ug[ug.shape[0] - SUBLANES:, :]
    sv_ref[...] = uv[uv.shape[0] - SUBLANES:, :]
    gate = conv(ug, cg_ref, cwg_ref, cbg_ref)
    val = conv(uv, cv_ref, cwv_ref, cbv_ref)
    hid = (_silu(gate) * val).astype(BF16)
    acc_ref[...] += _dot(hid, wd_ref[...])

    @pl.when(j == pl.num_programs(1) - 1)
    def _():
        o_ref[...] = x_ref[...] + gate_ref[...] * _rms(acc_ref[...], gpost_ref[...])


def _ffn_prompt(x, g_pre, mod, w_up, conv_w, conv_b, w_down, g_post, rows_per_seq, tm, tf):
    m, d = x.shape
    f = w_down.shape[0]
    nf = f // tf
    nseq = m // rows_per_seq
    tiles_per_seq = rows_per_seq // tm
    sc, sc_spec = _mod_spec(mod, 4, tm, rows_per_seq, d)
    sh, sh_spec = _mod_spec(mod, 3, tm, rows_per_seq, d)
    gate, gate_spec = _mod_spec(mod, 5, tm, rows_per_seq, d)
    state_spec = pl.BlockSpec((None, SUBLANES, tf), lambda i, j: (i, 0, j))
    out, sg, sv = pl.pallas_call(
        functools.partial(_ffn_prompt_kernel, tiles_per_seq=tiles_per_seq),
        grid=(m // tm, nf),
        in_specs=[pl.BlockSpec((tm, d), lambda i, j: (i, 0)),
                  pl.BlockSpec((1, d), lambda i, j: (0, 0)),
                  sc_spec, sh_spec,
                  pl.BlockSpec((d, tf), lambda i, j: (0, j)),
                  pl.BlockSpec((d, tf), lambda i, j: (0, j + nf)),
                  pl.BlockSpec((FFN_CONV, tf), lambda i, j: (0, j)),
                  pl.BlockSpec((FFN_CONV, tf), lambda i, j: (0, j + nf)),
                  pl.BlockSpec((1, tf), lambda i, j: (0, j)),
                  pl.BlockSpec((1, tf), lambda i, j: (0, j + nf)),
                  pl.BlockSpec((tf, d), lambda i, j: (j, 0)),
                  pl.BlockSpec((1, d), lambda i, j: (0, 0)),
                  gate_spec],
        out_specs=[pl.BlockSpec((tm, d), lambda i, j: (i, 0)), state_spec, state_spec],
        out_shape=[jax.ShapeDtypeStruct((m, d), F32),
                   jax.ShapeDtypeStruct((m // tm, SUBLANES, f), F32),
                   jax.ShapeDtypeStruct((m // tm, SUBLANES, f), F32)],
        scratch_shapes=[pltpu.VMEM((tm, d), BF16), pltpu.VMEM((tm, d), F32),
                        pltpu.VMEM((nf, SUBLANES, tf), F32), pltpu.VMEM((nf, SUBLANES, tf), F32)],
        compiler_params=_cparams("arbitrary", "arbitrary"),
        name="ffn_prompt",
    )(x, g_pre.reshape(1, d), sc, sh, w_up, w_up, conv_w, conv_w, conv_b.reshape(1, 2 * f),
      conv_b.reshape(1, 2 * f), w_down, g_post.reshape(1, d), gate)
    state = jnp.concatenate([sg, sv], axis=-1)[tiles_per_seq - 1::tiles_per_seq, SUBLANES - (FFN_CONV - 1):, :]
    return out, state


def _ffn_sample_kernel(x_ref, g_ref, sc_ref, sh_ref, wg_ref, wv_ref, cwg_ref, cwv_ref, cbg_ref, cbv_ref,
                       p0g_ref, p0v_ref, p1g_ref, p1v_ref, wd_ref, gpost_ref, gate_ref,
                       o_ref, ug_ref, uv_ref, h_ref, acc_ref):
    j = pl.program_id(1)

    @pl.when(j == 0)
    def _():
        h = _rms(x_ref[...], g_ref[...]) * (1.0 + sc_ref[...]) + sh_ref[...]
        h_ref[...] = h.astype(BF16)
        acc_ref[...] = jnp.zeros_like(acc_ref)

    ug = _dot(h_ref[...], wg_ref[...])
    uv = _dot(h_ref[...], wv_ref[...])
    ug_ref[...] = ug
    uv_ref[...] = uv
    gate = cwg_ref[0:1, :] * p0g_ref[...] + cwg_ref[1:2, :] * p1g_ref[...] + cwg_ref[2:3, :] * ug + cbg_ref[...]
    val = cwv_ref[0:1, :] * p0v_ref[...] + cwv_ref[1:2, :] * p1v_ref[...] + cwv_ref[2:3, :] * uv + cbv_ref[...]
    hid = (_silu(gate) * val).astype(BF16)
    acc_ref[...] += _dot(hid, wd_ref[...])

    @pl.when(j == pl.num_programs(1) - 1)
    def _():
        o_ref[...] = x_ref[...] + gate_ref[...] * _rms(acc_ref[...], gpost_ref[...])


def _ffn_sample(x, g_pre, mod, w_up, conv_w, conv_b, w_down, g_post, prev, tf):
    m, d = x.shape
    f = w_down.shape[0]
    nf = f // tf
    tm = m
    sc, sc_spec = _mod_spec(mod, 4, tm, 1, d)
    sh, sh_spec = _mod_spec(mod, 3, tm, 1, d)
    gate, gate_spec = _mod_spec(mod, 5, tm, 1, d)
    prev2 = prev.reshape(m, (FFN_CONV - 1) * 2 * f)
    up_spec = pl.BlockSpec((tm, tf), lambda i, j: (i, j))
    out, up_g, up_v = pl.pallas_call(
        _ffn_sample_kernel,
        grid=(1, nf),
        in_specs=[pl.BlockSpec((tm, d), lambda i, j: (i, 0)),
                  pl.BlockSpec((1, d), lambda i, j: (0, 0)),
                  sc_spec, sh_spec,
                  pl.BlockSpec((d, tf), lambda i, j: (0, j)),
                  pl.BlockSpec((d, tf), lambda i, j: (0, j + nf)),
                  pl.BlockSpec((FFN_CONV, tf), lambda i, j: (0, j)),
                  pl.BlockSpec((FFN_CONV, tf), lambda i, j: (0, j + nf)),
                  pl.BlockSpec((1, tf), lambda i, j: (0, j)),
                  pl.BlockSpec((1, tf), lambda i, j: (0, j + nf)),
                  pl.BlockSpec((tm, tf), lambda i, j: (i, j)),
                  pl.BlockSpec((tm, tf), lambda i, j: (i, j + nf)),
                  pl.BlockSpec((tm, tf), lambda i, j: (i, j + 2 * nf)),
                  pl.BlockSpec((tm, tf), lambda i, j: (i, j + 3 * nf)),
                  pl.BlockSpec((tf, d), lambda i, j: (j, 0)),
                  pl.BlockSpec((1, d), lambda i, j: (0, 0)),
                  gate_spec],
        out_specs=[pl.BlockSpec((tm, d), lambda i, j: (i, 0)), up_spec, up_spec],
        out_shape=[jax.ShapeDtypeStruct((m, d), F32), jax.ShapeDtypeStruct((m, f), F32),
                   jax.ShapeDtypeStruct((m, f), F32)],
        scratch_shapes=[pltpu.VMEM((tm, d), BF16), pltpu.VMEM((tm, d), F32)],
        compiler_params=_cparams("arbitrary", "arbitrary"),
        name="ffn_sample",
    )(x, g_pre.reshape(1, d), sc, sh, w_up, w_up, conv_w, conv_w, conv_b.reshape(1, 2 * f),
      conv_b.reshape(1, 2 * f), prev2, prev2, prev2, prev2, w_down, g_post.reshape(1, d), gate)
    return out, jnp.concatenate([up_g, up_v], axis=-1)


def _tri(n, fn):
    r = lax.broadcasted_iota(jnp.int32, (n, n), 0)
    c = lax.broadcasted_iota(jnp.int32, (n, n), 1)
    return jnp.where(fn(r, c), 1.0, 0.0).astype(BF16)


def _dot_exact01(m01, x):
    hi, mid, lo = _split3(x)
    return _dot(m01, hi) + _dot(m01, mid) + _dot(m01, lo)


def _fox_gate_kernel(s_ref, b_ref, lf_ref, cum_ref, rep_ref, carry_ref):
    @pl.when(pl.program_id(1) == 0)
    def _():
        carry_ref[...] = jnp.zeros_like(carry_ref)

    lf = _log_sigmoid(s_ref[...] + b_ref[...])
    n = lf.shape[0]
    cum = _dot_exact01(_tri(n, lambda r, c: r >= c), lf) + carry_ref[0:1, :]
    lf_ref[...] = lf
    cum_ref[...] = cum
    for h in range(N_HEADS):
        rep_ref[:, h * LANES:(h + 1) * LANES] = jnp.broadcast_to(cum[:, S_FOX_F + h:S_FOX_F + h + 1], (n, LANES))
    carry_ref[...] = jnp.broadcast_to(cum[n - 1:n, :], carry_ref.shape)


def _fox_gates(pc, b_row, nseq, t, tc=256):
    nt = t // tc
    cb = C_SMALL // LANES
    spec = pl.BlockSpec((tc, LANES), lambda b, i: (b * nt + i, 0))
    return pl.pallas_call(
        _fox_gate_kernel,
        grid=(nseq, nt),
        in_specs=[pl.BlockSpec((tc, LANES), lambda b, i: (b * nt + i, cb)),
                  pl.BlockSpec((1, LANES), lambda b, i: (0, 0))],
        out_specs=[spec, spec, pl.BlockSpec((tc, N_HEADS * LANES), lambda b, i: (b * nt + i, 0))],
        out_shape=[jax.ShapeDtypeStruct((nseq * t, LANES), F32)] * 2
        + [jax.ShapeDtypeStruct((nseq * t, N_HEADS * LANES), F32)],
        scratch_shapes=[pltpu.VMEM((SUBLANES, LANES), F32)],
        compiler_params=_cparams("parallel", "arbitrary"),
        name="fox_gates",
    )(pc, b_row)


MASKED = -1e30


def _flash_kernel(*refs, mode, tq, tk):
    if mode == 'fox':
        q_ref, k_ref, v_ref, cq_ref, ck_ref, o_ref, m_ref, acc_ref = refs
    else:
        q_ref, k_ref, v_ref, bias_ref, sc_ref, thr_ref, o_ref, m_ref, acc_ref = refs
    qi = pl.program_id(2)
    ki = pl.program_id(3)

    @pl.when(ki == 0)
    def _():
        m_ref[...] = jnp.full_like(m_ref, MASKED)
        acc_ref[...] = jnp.zeros_like(acc_ref)

    @pl.when(ki <= qi)
    def _():
        t_pos = qi * tq + lax.broadcasted_iota(jnp.int32, (tq, LANES), 0)
        lane = lax.broadcasted_iota(jnp.int32, (tq, LANES), 1)
        masks = []
        for c in range(tk // LANES):
            mk = (ki * tk + c * LANES + lane) <= t_pos
            if mode == 'dsa':
                mk = mk & (sc_ref[:, c * LANES:(c + 1) * LANES] >= thr_ref[...])
            masks.append(mk)
        kb = k_ref[...].astype(BF16)
        v_ext = jnp.concatenate([v_ref[...].astype(BF16), jnp.ones((tk, LANES), BF16)], axis=1)
        for g in range(2):
            q = q_ref[:, g * HEAD_DIM:(g + 1) * HEAD_DIM].astype(BF16)
            s = _dot_nt(q, kb) * HEAD_DIM ** -0.5
            blocks = []
            for c, mk in enumerate(masks):
                blk = s[:, c * LANES:(c + 1) * LANES]
                if mode == 'fox':
                    blk = blk + (cq_ref[:, g * LANES:(g + 1) * LANES] - ck_ref[g:g + 1, c * LANES:(c + 1) * LANES])
                else:
                    blk = blk + bias_ref[g, :, c * LANES:(c + 1) * LANES]
                blocks.append(jnp.where(mk, blk, MASKED))
            top = functools.reduce(jnp.maximum, blocks)
            m_old = m_ref[g]
            m_new = jnp.maximum(m_old, jnp.max(top, axis=-1, keepdims=True))
            alpha = jnp.exp(m_old - m_new)
            p = jnp.concatenate([jnp.where(mk, jnp.exp(blk - m_new), 0.0) for blk, mk in zip(blocks, masks)], axis=1)
            pv = _dot(p.astype(BF16), v_ext)
            acc_ref[g, :, 0:LANES] = alpha * acc_ref[g, :, 0:LANES] + pv[:, 0:LANES]
            acc_ref[g, :, LANES:2 * LANES] = alpha * acc_ref[g, :, LANES:2 * LANES] + pv[:, LANES:2 * LANES]
            m_ref[g] = m_new

    @pl.when(ki == qi)
    def _():
        for g in range(2):
            o_ref[:, g * HEAD_DIM:(g + 1) * HEAD_DIM] = acc_ref[g, :, 0:LANES] / acc_ref[g, :, LANES:2 * LANES]


def _flash(pc, nseq, t, mode, extra, tq=ATTN_TILE):
    tk = tq
    nq = t // tq
    c_q, c_k, c_v = (C_FOX_Q, C_FOX_K, C_FOX_V) if mode == 'fox' else (C_DSA_Q, C_DSA_K, C_DSA_V)
    qb, kb, vb = c_q // 256, c_k // HEAD_DIM, c_v // HEAD_DIM
    in_specs = [pl.BlockSpec((tq, 256), lambda b, h, i, j: (b * nq + i, qb + h)),
                pl.BlockSpec((tk, HEAD_DIM), lambda b, h, i, j: (b * nq + jnp.minimum(i, j), kb + h)),
                pl.BlockSpec((tk, HEAD_DIM), lambda b, h, i, j: (b * nq + jnp.minimum(i, j), vb + h))]
    if mode == 'fox':
        in_specs += [pl.BlockSpec((tq, 2 * LANES), lambda b, h, i, j: (b * nq + i, h)),
                     pl.BlockSpec((None, 2, tk), lambda b, h, i, j: (b * N_KV + h, 0, jnp.minimum(i, j)))]
    else:
        in_specs += [pl.BlockSpec((None, 2, None, tq, tk),
                                  lambda b, h, i, j: (h, 0, jnp.clip(i - j, 0, 2), 0, 0)),
                     pl.BlockSpec((tq, tk), lambda b, h, i, j: (b * nq + i, jnp.minimum(i, j))),
                     pl.BlockSpec((tq, LANES), lambda b, h, i, j: (b * nq + i, 0))]
    return pl.pallas_call(
        functools.partial(_flash_kernel, mode=mode, tq=tq, tk=tk),
        grid=(nseq, N_KV, nq, nq),
        in_specs=in_specs,
        out_specs=pl.BlockSpec((tq, 256), lambda b, h, i, j: (b * nq + i, h)),
        out_shape=jax.ShapeDtypeStruct((nseq * t, N_HEADS * HEAD_DIM), F32),
        scratch_shapes=[pltpu.VMEM((2, tq, LANES), F32), pltpu.VMEM((2, tq, 2 * LANES), F32)],
        compiler_params=_cparams("parallel", "parallel", "parallel", "arbitrary"),
        name="flash_" + mode,
    )(pc, pc, pc, *extra)


def _split2(x):
    hi = x.astype(BF16)
    return hi, (x - hi.astype(F32)).astype(BF16)


def _sb_kernel(q_ref, k_ref, v_ref, o_ref, r_ref, acc_ref, *, tq, tk):
    qi = pl.program_id(2)
    ki = pl.program_id(3)

    @pl.when(ki == 0)
    def _():
        r_ref[...] = jnp.zeros_like(r_ref)
        acc_ref[...] = jnp.zeros_like(acc_ref)

    @pl.when(ki <= qi)
    def _():
        t_pos = qi * tq + lax.broadcasted_iota(jnp.int32, (tq, tk), 0)
        s_pos = (qi - ki) * tk + lax.broadcasted_iota(jnp.int32, (tq, tk), 1)
        mask = s_pos < t_pos
        after = jnp.concatenate([_tri(LANES, lambda r, c: r > c), jnp.ones((LANES, LANES), BF16)], axis=1)
        kb = k_ref[...].astype(BF16)
        vb = v_ref[...].astype(BF16)
        nblk = tk // LANES
        for g in range(2):
            q = q_ref[:, g * HEAD_DIM:(g + 1) * HEAD_DIM].astype(BF16)
            z = _dot_nt(q, kb) * HEAD_DIM ** -0.5
            sp = jnp.log(1.0 + jnp.exp(-jnp.abs(z)))
            log_take = jnp.minimum(z, 0.0) - sp
            log_stay = jnp.where(mask, jnp.minimum(-z, 0.0) - sp, 0.0)
            carry = r_ref[g]
            later = [None] * nblk
            for c in reversed(range(nblk)):
                hi, lo = _split2(log_stay[:, c * LANES:(c + 1) * LANES])
                sums = _dot(hi, after) + _dot(lo, after)
                later[c] = sums[:, 0:LANES] + carry
                carry = carry + sums[:, LANES:2 * LANES]
            a = jnp.where(mask, jnp.exp(log_take + jnp.concatenate(later, axis=1)), 0.0)
            acc_ref[g] += _dot(a.astype(BF16), vb)
            r_ref[g] = carry

    @pl.when(ki == qi)
    def _():
        for g in range(2):
            o_ref[:, g * HEAD_DIM:(g + 1) * HEAD_DIM] = acc_ref[g]


def _sb_attention(pc, nseq, t, tq=ATTN_TILE):
    tk = tq
    nq = t // tq
    qb, kb, vb = C_SB_Q // 256, C_SB_K // HEAD_DIM, C_SB_V // HEAD_DIM
    return pl.pallas_call(
        functools.partial(_sb_kernel, tq=tq, tk=tk),
        grid=(nseq, N_KV, nq, nq),
        in_specs=[pl.BlockSpec((tq, 256), lambda b, h, i, j: (b * nq + i, qb + h)),
                  pl.BlockSpec((tk, HEAD_DIM), lambda b, h, i, j: (b * nq + jnp.maximum(i - j, 0), kb + h)),
                  pl.BlockSpec((tk, HEAD_DIM), lambda b, h, i, j: (b * nq + jnp.maximum(i - j, 0), vb + h))],
        out_specs=pl.BlockSpec((tq, 256), lambda b, h, i, j: (b * nq + i, h)),
        out_shape=jax.ShapeDtypeStruct((nseq * t, N_HEADS * HEAD_DIM), F32),
        scratch_shapes=[pltpu.VMEM((2, tq, LANES), F32), pltpu.VMEM((2, tq, HEAD_DIM), F32)],
        compiler_params=_cparams("parallel", "parallel", "parallel", "arbitrary"),
        name="sb_attention",
    )(pc, pc, pc)


KEY_NEG_INF = -2139095041


def _float_key(x):
    b = pltpu.bitcast(x, jnp.int32)
    return b ^ ((b >> 31) & 0x7FFFFFFF)


def _key_float(k):
    return pltpu.bitcast(k ^ ((k >> 31) & 0x7FFFFFFF), F32)


def _kth_largest_key(count_ge, rows, k):
    def body(i, res):
        bit = 31 - i
        cand = jnp.where(bit == 31, jnp.zeros_like(res), res | jnp.left_shift(1, jnp.minimum(bit, 30)))
        return jnp.where(count_ge(cand) >= k, cand, res)

    res = jnp.full((rows, 1), jnp.iinfo(jnp.int32).min, jnp.int32)
    return lax.fori_loop(0, 32, body, res)


def _dsa_score_kernel(qi_ref, ki_ref, w_ref, sc_ref, thr_ref, key_ref, wrep_ref, *, tq, tk, topk):
    qi = pl.program_id(1)
    ki = pl.program_id(2)

    @pl.when((qi == 0) & (ki == 0))
    def _():
        key_ref[...] = jnp.full(key_ref.shape, jnp.iinfo(jnp.int32).min, jnp.int32)

    @pl.when(ki == 0)
    def _():
        hi, mid, lo = _split3(w_ref[...] * (IDX_DIM ** -0.5 * IDX_HEADS ** -0.5))
        r = lax.broadcasted_iota(jnp.int32, (LANES, LANES), 0)
        for h in range(IDX_HEADS):
            pick = jnp.where(r == S_DSA_W + h, 1.0, 0.0).astype(BF16)
            wrep_ref[h] = _dot(hi, pick) + _dot(mid, pick) + _dot(lo, pick)

    @pl.when(ki <= qi)
    def _():
        kidx = ki_ref[...].astype(BF16)
        nblk = tk // LANES
        blocks = [jnp.zeros((tq, LANES), F32)] * nblk
        for h in range(IDX_HEADS):
            q = qi_ref[:, h * IDX_DIM:(h + 1) * IDX_DIM].astype(BF16)
            si = _dot_nt(q, kidx)
            w = wrep_ref[h]
            blocks = [blk + jnp.maximum(si[:, c * LANES:(c + 1) * LANES], 0.0) * w for c, blk in enumerate(blocks)]
        score = jnp.concatenate(blocks, axis=1)
        t_pos = qi * tq + lax.broadcasted_iota(jnp.int32, (tq, tk), 0)
        s_pos = ki * tk + lax.broadcasted_iota(jnp.int32, (tq, tk), 1)
        score = jnp.where(s_pos <= t_pos, score, NEG_INF)
        sc_ref[...] = score
        key_ref[:, pl.ds(pl.multiple_of(ki * tk, tk), tk)] = _float_key(score)

    def search(width):
        def count_ge(cand):
            parts = []
            for r0 in range(0, tq, LANES):
                cand_r = cand[r0:r0 + LANES, :]
                cnt = jnp.zeros((LANES, LANES), jnp.int32)
                for c0 in range(0, width, LANES):
                    cnt = cnt + jnp.where(key_ref[r0:r0 + LANES, c0:c0 + LANES] >= cand_r, 1, 0)
                parts.append(jnp.sum(cnt, axis=-1, keepdims=True))
            return jnp.concatenate(parts, axis=0)

        res = jnp.maximum(_kth_largest_key(count_ge, tq, topk), KEY_NEG_INF)
        thr_ref[...] = jnp.broadcast_to(_key_float(res), thr_ref.shape)

    total = key_ref.shape[1]
    widths = sorted({max(tk, (total * n // 4) // tk * tk) for n in (1, 2, 3, 4)})
    for n, width in enumerate(widths):
        lo = widths[n - 1] if n else 0

        @pl.when((ki == qi) & ((qi + 1) * tk > lo) & ((qi + 1) * tk <= width))
        def _(width=width):
            search(width)


def _dsa_scores(pc, nseq, t, topk, tq=256):
    tk = tq
    nq = t // tq
    return pl.pallas_call(
        functools.partial(_dsa_score_kernel, tq=tq, tk=tk, topk=topk),
        grid=(nseq, nq, nq),
        in_specs=[pl.BlockSpec((tq, IDX_HEADS * IDX_DIM), lambda b, i, j: (b * nq + i, C_DSA_QIDX // 2048)),
                  pl.BlockSpec((tk, IDX_DIM), lambda b, i, j: (b * nq + jnp.minimum(i, j), C_DSA_KIDX // IDX_DIM)),
                  pl.BlockSpec((tq, LANES), lambda b, i, j: (b * nq + i, C_SMALL // LANES))],
        out_specs=[pl.BlockSpec((tq, tk), lambda b, i, j: (b * nq + i, jnp.minimum(i, j))),
                   pl.BlockSpec((tq, LANES), lambda b, i, j: (b * nq + i, 0))],
        out_shape=[jax.ShapeDtypeStruct((nseq * t, t), F32), jax.ShapeDtypeStruct((nseq * t, LANES), F32)],
        scratch_shapes=[pltpu.VMEM((tq, t), jnp.int32), pltpu.VMEM((IDX_HEADS, tq, LANES), F32)],
        compiler_params=_cparams("arbitrary", "arbitrary", "arbitrary"),
        name="dsa_scores",
    )(pc, pc, pc)


def _t5_bucket(rel):
    n = jnp.maximum(rel, 0)
    max_exact = N_BUCKETS // 2
    nf = jnp.maximum(n, 1).astype(F32)
    large = max_exact + (jnp.log(nf / max_exact) / math.log(MAX_DISTANCE / max_exact)
                         * (N_BUCKETS - max_exact)).astype(jnp.int32)
    large = jnp.minimum(large, N_BUCKETS - 1)
    return jnp.where(n < max_exact, n, large)


def _rel_bias_tiles(rel_bias, tq):
    i = jnp.arange(tq, dtype=jnp.int32)[:, None]
    j = jnp.arange(tq, dtype=jnp.int32)[None, :]
    rel = jnp.stack([i - j, tq + i - j, 2 * tq + i - j])
    tiles = rel_bias[_t5_bucket(rel)].astype(F32)
    tiles = jnp.transpose(tiles, (3, 0, 1, 2))
    return tiles.reshape(N_KV, 2, 3, tq, tq)


GDN_W = N_HEADS * HEAD_DIM


def _softplus(x):
    return jnp.maximum(x, 0.0) + jnp.log(1.0 + jnp.exp(-jnp.abs(x)))


def _l2norm_heads(x, scale):
    parts = []
    for h in range(N_HEADS):
        xh = x[:, h * HEAD_DIM:(h + 1) * HEAD_DIM]
        parts.append(xh * (lax.rsqrt(jnp.sum(xh * xh, axis=-1, keepdims=True) + EPS) * scale))
    return jnp.concatenate(parts, axis=-1)


def _gdn_gates(small, alog_row, dt_row):
    glog = -jnp.exp(alog_row) * _softplus(small + dt_row)
    beta = 1.0 / (1.0 + jnp.exp(-small))
    lane = lax.broadcasted_iota(jnp.int32, (small.shape[0], LANES), 1)
    blocks = []
    for h in range(N_HEADS):
        g_col = glog[:, S_GDN_A + h:S_GDN_A + h + 1]
        b_col = beta[:, S_GDN_B + h:S_GDN_B + h + 1]
        blocks.append(jnp.where(lane == 0, g_col, jnp.where(lane == 1, b_col, 0.0)))
    return jnp.concatenate(blocks, axis=-1)


def _gdn_prep_prompt_kernel(uq_ref, uk_ref, uv_ref, cw_ref, small_ref, alog_ref, dt_ref,
                            q_ref, k_ref, v_ref, gb_ref, sq_ref, sk_ref, sv_ref, carry_ref):
    @pl.when(pl.program_id(1) == 0)
    def _():
        carry_ref[...] = jnp.zeros_like(carry_ref)

    def conv(idx, u_ref, s_ref):
        u = u_ref[...]
        prev = carry_ref[idx]
        cw = cw_ref[:, idx * GDN_W:(idx + 1) * GDN_W]
        out = cw[GDN_CONV - 1:GDN_CONV, :] * u
        for j in range(GDN_CONV - 1):
            out = out + cw[j:j + 1, :] * _shift_rows(u, prev, GDN_CONV - 1 - j)
        last = u[u.shape[0] - SUBLANES:, :]
        carry_ref[idx] = last
        s_ref[...] = last
        return _silu(out)

    q_ref[...] = _l2norm_heads(conv(0, uq_ref, sq_ref), HEAD_DIM ** -0.5)
    k_ref[...] = _l2norm_heads(conv(1, uk_ref, sk_ref), 1.0)
    v_ref[...] = conv(2, uv_ref, sv_ref)
    gb_ref[...] = _gdn_gates(small_ref[...], alog_ref[...], dt_ref[...])


def _gdn_prep_prompt(pc, conv_w, alog_row, dt_row, nseq, t, tm=256):
    nt = t // tm
    ub = C_GDN_QKV // GDN_W
    row_spec = pl.BlockSpec((tm, GDN_W), lambda b, i: (b * nt + i, 0))
    st_spec = pl.BlockSpec((None, SUBLANES, GDN_W), lambda b, i: (b, 0, 0))
    par_spec = pl.BlockSpec((1, LANES), lambda b, i: (0, 0))
    outs = pl.pallas_call(
        _gdn_prep_prompt_kernel,
        grid=(nseq, nt),
        in_specs=[pl.BlockSpec((tm, GDN_W), lambda b, i: (b * nt + i, ub)),
                  pl.BlockSpec((tm, GDN_W), lambda b, i: (b * nt + i, ub + 1)),
                  pl.BlockSpec((tm, GDN_W), lambda b, i: (b * nt + i, ub + 2)),
                  pl.BlockSpec((GDN_CONV, 3 * GDN_W), lambda b, i: (0, 0)),
                  pl.BlockSpec((tm, LANES), lambda b, i: (b * nt + i, C_SMALL // LANES)),
                  par_spec, par_spec],
        out_specs=[row_spec, row_spec, row_spec, row_spec, st_spec, st_spec, st_spec],
        out_shape=[jax.ShapeDtypeStruct((nseq * t, GDN_W), F32)] * 4
        + [jax.ShapeDtypeStruct((nseq, SUBLANES, GDN_W), F32)] * 3,
        scratch_shapes=[pltpu.VMEM((3, SUBLANES, GDN_W), F32)],
        compiler_params=_cparams("parallel", "arbitrary"),
        name="gdn_prep_prompt",
    )(pc, pc, pc, conv_w, pc, alog_row, dt_row)
    q, k, v, gb, sq, sk, sv = outs
    conv_state = jnp.concatenate([sq, sk, sv], axis=-1)[:, SUBLANES - (GDN_CONV - 1):, :]
    return q, k, v, gb, conv_state


def _dot_3pass(a, b):
    a_hi, a_lo = _split2(a)
    b_hi, b_lo = _split2(b)
    return _dot(a_hi, b_hi) + (_dot(a_hi, b_lo) + _dot(a_lo, b_hi))


def _gdn_chunk_kernel(q_ref, k_ref, v_ref, gb_ref, z_ref, nw_ref, o_ref, sfin_ref, s_ref, *, chunks, heads):
    c = GDN_CHUNK

    @pl.when(pl.program_id(2) == 0)
    def _():
        s_ref[...] = jnp.zeros_like(s_ref)

    n = chunks * c
    r = lax.broadcasted_iota(jnp.int32, (n, n), 0)
    col = lax.broadcasted_iota(jnp.int32, (n, n), 1)
    same = (r // c) == (col // c)
    lower = same & (r >= col)
    strict = same & (r > col)
    eye = jnp.where(r == col, 1.0, 0.0)
    low01 = jnp.where(lower, 1.0, 0.0).astype(BF16)
    up01 = jnp.where(same & (col > r), 1.0, 0.0).astype(BF16)

    def chunk_terms(hh):
        cols = slice(hh * HEAD_DIM, (hh + 1) * HEAD_DIM)
        q = q_ref[:, cols]
        k = k_ref[:, cols]
        v = v_ref[:, cols]
        g = gb_ref[:, hh * HEAD_DIM:hh * HEAD_DIM + 1]
        beta = gb_ref[:, hh * HEAD_DIM + 1:hh * HEAD_DIM + 2]
        g_lanes = jnp.broadcast_to(g, (n, HEAD_DIM))
        gc = _dot_exact01(low01, g_lanes)
        rest = _dot_exact01(up01, g_lanes)
        diff = _dot_exact01(low01, jnp.where(strict, jnp.broadcast_to(g, (n, n)), 0.0))
        decay = jnp.where(lower, jnp.exp(jnp.where(lower, diff, 0.0)), 0.0)
        kb = k * beta
        lmat = jnp.where(strict, _dot_nt(kb.astype(BF16), k.astype(BF16)) * decay, 0.0)
        p = -lmat
        tinv = eye + p
        for _ in range(5):
            p = _dot_3pass(p, p)
            tinv = tinv + _dot_3pass(tinv, p)
        tb = tinv.astype(BF16)
        u = _dot(tb, (v * beta).astype(BF16))
        w = _dot(tb, (kb * jnp.exp(gc)).astype(BF16)).astype(BF16)
        aqk = jnp.where(lower, _dot_nt(q.astype(BF16), k.astype(BF16)) * decay, 0.0).astype(BF16)
        q_dec = (q * jnp.exp(gc)).astype(BF16)
        k_dec = (k * jnp.exp(rest)).astype(BF16)
        return u, w, aqk, q_dec, k_dec, jnp.exp(gc)

    terms = [chunk_terms(hh) for hh in range(heads)]
    states = [s_ref[hh] for hh in range(heads)]
    for ci in range(chunks):
        rows = slice(ci * c, (ci + 1) * c)
        for hh in range(heads):
            cols = slice(hh * HEAD_DIM, (hh + 1) * HEAD_DIM)
            u, w, aqk, q_dec, k_dec, chunk_decay = terms[hh]
            s = states[hh]
            sb = s.astype(BF16)
            v_new = u[rows, :] - _dot(w[rows, :], sb)
            o = _dot(q_dec[rows, :], sb) + _dot(aqk[rows, rows], v_new.astype(BF16))
            states[hh] = (s * chunk_decay[(ci + 1) * c - 1:(ci + 1) * c, :]
                          + _dot_tn(k_dec[rows, :], v_new.astype(BF16)))
            o_ref[rows, cols] = _rms(o, nw_ref[...]) * _silu(z_ref[rows, cols])
    for hh in range(heads):
        s_ref[hh] = states[hh]

    @pl.when(pl.program_id(2) == pl.num_programs(2) - 1)
    def _():
        sfin_ref[...] = s_ref[...]


def _gdn_chunked(q, k, v, gb, pc, norm_w, nseq, t, tm=256, heads=2):
    nt = t // tm
    width = heads * HEAD_DIM
    zb = C_GDN_Z // width
    head_spec = pl.BlockSpec((tm, width), lambda b, h, i: (b * nt + i, h))
    return pl.pallas_call(
        functools.partial(_gdn_chunk_kernel, chunks=tm // GDN_CHUNK, heads=heads),
        grid=(nseq, N_HEADS // heads, nt),
        in_specs=[head_spec, head_spec, head_spec, head_spec,
                  pl.BlockSpec((tm, width), lambda b, h, i: (b * nt + i, zb + h)),
                  pl.BlockSpec((1, HEAD_DIM), lambda b, h, i: (0, 0))],
        out_specs=[head_spec,
                   pl.BlockSpec((None, heads, HEAD_DIM, HEAD_DIM), lambda b, h, i: (b, h, 0, 0))],
        out_shape=[jax.ShapeDtypeStruct((nseq * t, GDN_W), F32),
                   jax.ShapeDtypeStruct((nseq, N_HEADS, HEAD_DIM, HEAD_DIM), F32)],
        scratch_shapes=[pltpu.VMEM((heads, HEAD_DIM, HEAD_DIM), F32)],
        compiler_params=_cparams("parallel", "parallel", "arbitrary"),
        name="gdn_chunked",
    )(q, k, v, gb, pc, norm_w.reshape(1, HEAD_DIM))


def _gdn_prep_sample_kernel(*refs):
    u_refs = refs[0:3]
    p_refs = refs[3:12]
    cw_ref, small_ref, alog_ref, dt_ref, bfox_ref = refs[12:17]
    q_ref, k_ref, v_ref, gb_ref, lf_ref = refs[17:22]

    def conv(idx):
        cw = cw_ref[:, idx * GDN_W:(idx + 1) * GDN_W]
        out = cw[GDN_CONV - 1:GDN_CONV, :] * u_refs[idx][...]
        for r in range(GDN_CONV - 1):
            out = out + cw[r:r + 1, :] * p_refs[3 * r + idx][...]
        return _silu(out)

    q_ref[...] = _l2norm_heads(conv(0), HEAD_DIM ** -0.5)
    k_ref[...] = _l2norm_heads(conv(1), 1.0)
    v_ref[...] = conv(2)
    gb_ref[...] = _gdn_gates(small_ref[...], alog_ref[...], dt_ref[...])
    lf_ref[...] = _log_sigmoid(small_ref[...] + bfox_ref[...])


def _gdn_prep_sample(pc, prev, conv_w, alog_row, dt_row, bfox_row):
    m = pc.shape[0]
    ub = C_GDN_QKV // GDN_W
    prev2 = prev.reshape(m, (GDN_CONV - 1) * 3 * GDN_W)
    row_spec = pl.BlockSpec((m, GDN_W), lambda i: (0, 0))
    par_spec = pl.BlockSpec((1, LANES), lambda i: (0, 0))
    in_specs = [pl.BlockSpec((m, GDN_W), functools.partial(lambda i, c: (0, c), c=ub + n)) for n in range(3)]
    in_specs += [pl.BlockSpec((m, GDN_W), functools.partial(lambda i, c: (0, c), c=n)) for n in range(9)]
    in_specs += [pl.BlockSpec((GDN_CONV, 3 * GDN_W), lambda i: (0, 0)),
                 pl.BlockSpec((m, LANES), lambda i: (0, C_SMALL // LANES)), par_spec, par_spec, par_spec]
    return pl.pallas_call(
        _gdn_prep_sample_kernel,
        grid=(1,),
        in_specs=in_specs,
        out_specs=[row_spec] * 4 + [pl.BlockSpec((m, LANES), lambda i: (0, 0))],
        out_shape=[jax.ShapeDtypeStruct((m, GDN_W), F32)] * 4 + [jax.ShapeDtypeStruct((m, LANES), F32)],
        compiler_params=_cparams("arbitrary"),
        name="gdn_prep_sample",
    )(pc, pc, pc, *([prev2] * 9), conv_w, pc, alog_row, dt_row, bfox_row)


def _gdn_step_kernel(q_ref, k_ref, v_ref, gb_ref, z_ref, nw_ref, s_ref, o_ref, so_ref, *, tb):
    pad = jnp.zeros((HEAD_DIM - tb, HEAD_DIM), F32)
    for h in range(N_HEADS):
        cols = slice(h * HEAD_DIM, (h + 1) * HEAD_DIM)
        k_t = jnp.concatenate([k_ref[:, cols], pad], axis=0).T
        q_t = jnp.concatenate([q_ref[:, cols], pad], axis=0).T
        for b in range(tb):
            decay = jnp.exp(gb_ref[b:b + 1, h * HEAD_DIM:h * HEAD_DIM + 1])
            beta = gb_ref[b:b + 1, h * HEAD_DIM + 1:h * HEAD_DIM + 2]
            kc = k_t[:, b:b + 1]
            s = s_ref[b, h] * decay
            v_old = jnp.sum(s * kc, axis=0, keepdims=True)
            s = s + kc * ((v_ref[b:b + 1, cols] - v_old) * beta)
            so_ref[b, h] = s
            o = jnp.sum(s * q_t[:, b:b + 1], axis=0, keepdims=True)
            o_ref[b:b + 1, cols] = _rms(o, nw_ref[...]) * _silu(z_ref[b:b + 1, cols])


def _gdn_step(q, k, v, gb, pc, norm_w, state, tb=8):
    m = q.shape[0]
    row_spec = pl.BlockSpec((tb, GDN_W), lambda i: (i, 0))
    st_spec = pl.BlockSpec((tb, N_HEADS, HEAD_DIM, HEAD_DIM), lambda i: (i, 0, 0, 0))
    return pl.pallas_call(
        functools.partial(_gdn_step_kernel, tb=tb),
        grid=(m // tb,),
        in_specs=[row_spec, row_spec, row_spec, row_spec,
                  pl.BlockSpec((tb, GDN_W), lambda i: (i, C_GDN_Z // GDN_W)),
                  pl.BlockSpec((1, HEAD_DIM), lambda i: (0, 0)), st_spec],
        out_specs=[row_spec, st_spec],
        out_shape=[jax.ShapeDtypeStruct((m, GDN_W), F32), jax.ShapeDtypeStruct(state.shape, F32)],
        compiler_params=_cparams("parallel"),
        name="gdn_step",
    )(q, k, v, gb, pc, norm_w.reshape(1, HEAD_DIM), state)


def _dec_idx_kernel(pt_ref, q_ref, w_ref, knew_ref, *refs, page):
    kidx_refs, (sc_ref, self_ref) = refs[:-2], refs[-2:]
    q = q_ref[...]
    qb = q.astype(BF16)
    w = w_ref[...] * (IDX_DIM ** -0.5 * IDX_HEADS ** -0.5)
    for pg, kidx_ref in enumerate(kidx_refs):
        si = _dot_nt(qb, kidx_ref[...].astype(BF16))
        sc_ref[:, pg * page:(pg + 1) * page] = jnp.sum(jnp.maximum(si, 0.0) * w, axis=0, keepdims=True)
    si_new = jnp.sum(q * knew_ref[...], axis=-1, keepdims=True)
    own = jnp.sum(jnp.maximum(si_new, 0.0) * w, axis=0, keepdims=True)
    self_ref[...] = jnp.broadcast_to(own, self_ref.shape)


def _dec_idx_scores(layer, page_table, qidx3, w3, knew3, cache_kidx):
    nb, npages = page_table.shape
    page = cache_kidx.shape[2]
    page_specs = [pl.BlockSpec((None, None, page, IDX_DIM),
                               functools.partial(lambda b, pt, pg: (layer, pt[b, pg], 0, 0), pg=pg))
                  for pg in range(npages)]
    grid_spec = pltpu.PrefetchScalarGridSpec(
        num_scalar_prefetch=1,
        grid=(nb,),
        in_specs=[pl.BlockSpec((None, IDX_HEADS, IDX_DIM), lambda b, pt: (b, 0, 0)),
                  pl.BlockSpec((None, IDX_HEADS, 1), lambda b, pt: (b, 0, 0)),
                  pl.BlockSpec((None, 1, IDX_DIM), lambda b, pt: (b, 0, 0))] + page_specs,
        out_specs=[pl.BlockSpec((None, 1, npages * page), lambda b, pt: (b, 0, 0)),
                   pl.BlockSpec((None, 1, LANES), lambda b, pt: (b, 0, 0))])
    return pl.pallas_call(
        functools.partial(_dec_idx_kernel, page=page),
        grid_spec=grid_spec,
        out_shape=[jax.ShapeDtypeStruct((nb, 1, npages * page), F32), jax.ShapeDtypeStruct((nb, 1, LANES), F32)],
        compiler_params=_cparams("parallel"),
        name="dec_idx_scores",
    )(page_table, qidx3, w3, knew3, *([cache_kidx] * npages))


def _dec_thr_kernel(sc_ref, self_ref, sel_ref, own_sel_ref, *, topk):
    scores = sc_ref[...]
    keys = _float_key(scores)
    own = _float_key(self_ref[:, 0:1])

    def count_ge(cand):
        cnt = jnp.sum(jnp.where(keys >= cand, 1, 0), axis=-1, keepdims=True)
        return cnt + jnp.where(own >= cand, 1, 0)

    res = jnp.maximum(_kth_largest_key(count_ge, keys.shape[0], topk), KEY_NEG_INF)
    thr = _key_float(res)
    own_sel_ref[...] = jnp.broadcast_to(jnp.where(self_ref[:, 0:1] >= thr, 1.0, 0.0), own_sel_ref.shape)
    r = lax.broadcasted_iota(jnp.int32, (LANES, N_KV * LANES), 0)
    c = lax.broadcasted_iota(jnp.int32, (LANES, N_KV * LANES), 1)
    spread = jnp.where(r == c // N_KV, 1.0, 0.0).astype(BF16)
    for j in range(scores.shape[1] // LANES):
        sel = jnp.where(scores[:, j * LANES:(j + 1) * LANES] >= thr, 1.0, 0.0).astype(BF16)
        sel_ref[:, j * N_KV * LANES:(j + 1) * N_KV * LANES] = _dot(sel, spread)


def _dec_threshold(scores, own, topk):
    nb, n = scores.shape
    return pl.pallas_call(
        functools.partial(_dec_thr_kernel, topk=topk),
        grid=(1,),
        in_specs=[pl.BlockSpec((nb, n), lambda i: (0, 0)), pl.BlockSpec((nb, LANES), lambda i: (0, 0))],
        out_specs=[pl.BlockSpec((nb, N_KV * n), lambda i: (0, 0)), pl.BlockSpec((nb, LANES), lambda i: (0, 0))],
        out_shape=[jax.ShapeDtypeStruct((nb, N_KV * n), F32), jax.ShapeDtypeStruct((nb, LANES), F32)],
        compiler_params=_cparams("arbitrary"),
        name="dec_threshold",
    )(scores, own)


N_CACHES = 6


def _dec_attn_kernel(pt_ref, fq_ref, sq_ref, dq_ref, fkn_ref, fvn_ref, dkn_ref, dvn_ref, lfn_ref, sel_ref, own_ref,
                     bias0_ref, bias_ref, fk_hbm, fv_hbm, sk_hbm, sv_hbm, dk_hbm, dv_hbm, lf_hbm,
                     of_ref, os_ref, od_ref, kv_buf, lf_buf, sems, *, layer, npages, page):
    b = pl.program_id(0)
    nb = pl.num_programs(0)
    rows = N_KV * page
    n = npages * rows
    caches = (fk_hbm, fv_hbm, sk_hbm, sv_hbm, dk_hbm, dv_hbm)
    scale = HEAD_DIM ** -0.5

    def page_copies(seq, slot):
        out = []
        for pg in range(npages):
            pid = pt_ref[seq, pg]
            for c, hbm in enumerate(caches):
                out.append(pltpu.make_async_copy(hbm.at[layer, pid], kv_buf.at[slot, c, pl.ds(pg * rows, rows)],
                                                 sems.at[slot, c]))
            out.append(pltpu.make_async_copy(lf_hbm.at[layer, pid], lf_buf.at[slot, pg], sems.at[slot, N_CACHES]))
        return out

    slot = b % 2

    @pl.when(b == 0)
    def _():
        for cp in page_copies(0, 0):
            cp.start()

    @pl.when(b + 1 < nb)
    def _():
        for cp in page_copies(b + 1, 1 - slot):
            cp.start()

    for cp in page_copies(b, slot):
        cp.wait()

    row = lax.broadcasted_iota(jnp.int32, (SUBLANES, n), 0)
    col = lax.broadcasted_iota(jnp.int32, (SUBLANES, n), 1)
    mine = (col % N_KV) == jnp.where(row < 2, 0, 1)
    top = lax.broadcasted_iota(jnp.int32, (SUBLANES, HEAD_DIM), 0) < 2

    def per_head(x2):
        return jnp.where(top, x2[0:1, :], x2[1:2, :])

    def slab(c):
        return kv_buf[slot, c].astype(BF16)

    rr = lax.broadcasted_iota(jnp.int32, (page, rows), 0)
    cc = lax.broadcasted_iota(jnp.int32, (page, rows), 1)
    pos_after = jnp.where(rr > cc // N_KV, 1.0, 0.0).astype(BF16)
    carry = lfn_ref[:, 0:1]
    decay = [None] * npages
    for pg in reversed(range(npages)):
        lf = lf_buf[slot, pg]
        hi, mid, lo = _split3(lf)
        decay[pg] = _dot(hi, pos_after) + _dot(mid, pos_after) + _dot(lo, pos_after) + carry
        carry = carry + jnp.sum(lf, axis=-1, keepdims=True)
    s = _dot_nt(fq_ref[...].astype(BF16), slab(0)) * scale + jnp.concatenate(decay, axis=-1)
    s = jnp.where(mine, s, MASKED)
    s_own = jnp.sum(fq_ref[...] * per_head(fkn_ref[...]), axis=-1, keepdims=True) * scale
    m = jnp.maximum(jnp.max(s, axis=-1, keepdims=True), s_own)
    w = jnp.where(mine, jnp.exp(s - m), 0.0)
    w_own = jnp.exp(s_own - m)
    acc = _dot(w.astype(BF16), slab(1)) + w_own * per_head(fvn_ref[...])
    of_ref[...] = acc / (jnp.sum(w, axis=-1, keepdims=True) + w_own)

    z = _dot_nt(sq_ref[...].astype(BF16), slab(2)) * scale
    sp = jnp.log(1.0 + jnp.exp(-jnp.abs(z)))
    log_take = jnp.minimum(z, 0.0) - sp
    log_stay = jnp.where(mine, jnp.minimum(-z, 0.0) - sp, 0.0)
    r2 = lax.broadcasted_iota(jnp.int32, (rows, rows), 0)
    c2 = lax.broadcasted_iota(jnp.int32, (rows, rows), 1)
    row_after = jnp.where(r2 // N_KV > c2 // N_KV, 1.0, 0.0).astype(BF16)
    carry = jnp.zeros((SUBLANES, 1), F32)
    later = [None] * npages
    for pg in reversed(range(npages)):
        ls = log_stay[:, pg * rows:(pg + 1) * rows]
        hi, lo = _split2(ls)
        later[pg] = _dot(hi, row_after) + _dot(lo, row_after) + carry
        carry = carry + jnp.sum(ls, axis=-1, keepdims=True)
    a = jnp.where(mine, jnp.exp(log_take + jnp.concatenate(later, axis=-1)), 0.0)
    os_ref[...] = _dot(a.astype(BF16), slab(3))

    keep = mine & (sel_ref[...] > 0.5)
    s = jnp.where(keep, _dot_nt(dq_ref[...].astype(BF16), slab(4)) * scale + bias_ref[...], MASKED)
    own_on = own_ref[:, 0:1] > 0.5
    d_own = jnp.sum(dq_ref[...] * per_head(dkn_ref[...]), axis=-1, keepdims=True) * scale + bias0_ref[:, 0:1]
    d_own = jnp.where(own_on, d_own, MASKED)
    m = jnp.maximum(jnp.max(s, axis=-1, keepdims=True), d_own)
    w = jnp.where(keep, jnp.exp(s - m), 0.0)
    w_own = jnp.where(own_on, jnp.exp(d_own - m), 0.0)
    acc = _dot(w.astype(BF16), slab(5)) + w_own * per_head(dvn_ref[...])
    od_ref[...] = acc / (jnp.sum(w, axis=-1, keepdims=True) + w_own)


def _dec_attention(layer, page_table, q8, new_kv, lfn8, sel3, own3, bias0, bias_dec, caches, lf_cache8):
    nb, npages = page_table.shape
    page = caches[0].shape[2] // N_KV
    n = npages * page * N_KV
    per_b8 = pl.BlockSpec((None, SUBLANES, HEAD_DIM), lambda b, pt: (b, 0, 0))
    per_b2 = pl.BlockSpec((None, N_KV, HEAD_DIM), lambda b, pt: (b, 0, 0))
    hbm = pl.BlockSpec(memory_space=pl.ANY)
    in_specs = [per_b8, per_b8, per_b8, per_b2, per_b2, per_b2, per_b2, per_b8,
                pl.BlockSpec((None, 1, n), lambda b, pt: (b, 0, 0)),
                pl.BlockSpec((None, 1, LANES), lambda b, pt: (b, 0, 0)),
                pl.BlockSpec((SUBLANES, LANES), lambda b, pt: (0, 0)),
                pl.BlockSpec((SUBLANES, n), lambda b, pt: (0, 0))] + [hbm] * (N_CACHES + 1)
    grid_spec = pltpu.PrefetchScalarGridSpec(
        num_scalar_prefetch=1, grid=(nb,), in_specs=in_specs,
        out_specs=[per_b8, per_b8, per_b8],
        scratch_shapes=[pltpu.VMEM((2, N_CACHES, n, HEAD_DIM), F32),
                        pltpu.VMEM((2, npages, SUBLANES, page), F32),
                        pltpu.SemaphoreType.DMA((2, N_CACHES + 1))])
    return pl.pallas_call(
        functools.partial(_dec_attn_kernel, layer=layer, npages=npages, page=page),
        grid_spec=grid_spec,
        out_shape=[jax.ShapeDtypeStruct((nb, SUBLANES, HEAD_DIM), F32)] * 3,
        compiler_params=_cparams("arbitrary"),
        name="dec_attention",
    )(page_table, *q8, *new_kv, lfn8, sel3, own3, bias0, bias_dec, *caches, lf_cache8)


def _lane_row(values, offset):
    return jnp.zeros((1, LANES), F32).at[0, offset:offset + values.shape[0]].set(values.astype(F32))


def _layer_params(l, w_mod, b_mod, g_mix_pre, g_mix_post, g_ffn_pre, g_ffn_post, w_in, b_fox_f, gdn_a_log,
                  gdn_dt_bias, gdn_conv_w, gdn_norm_w, w_out, w_up, ffn_conv_w, ffn_conv_b, w_down):
    segments, n_real = _in_proj_permutation(w_in.shape[1])
    parts = [w_in[l][:, s:s + n].astype(BF16) for s, n in segments]
    w_in_p = jnp.concatenate(parts + [jnp.zeros((w_in.shape[1], D_IN_PAD - n_real), BF16)], axis=1)
    return dict(
        w_mod=w_mod[l], b_mod=b_mod[l], g_mix_pre=g_mix_pre[l], g_mix_post=g_mix_post[l],
        g_ffn_pre=g_ffn_pre[l], g_ffn_post=g_ffn_post[l], w_in=w_in_p,
        b_fox_row=_lane_row(b_fox_f[l], S_FOX_F), alog_row=_lane_row(gdn_a_log[l], S_GDN_A),
        dt_row=_lane_row(gdn_dt_bias[l], S_GDN_A), gdn_conv_w=gdn_conv_w[l], gdn_norm_w=gdn_norm_w[l],
        w_out=w_out[l].astype(BF16), w_up=w_up[l].astype(BF16), ffn_conv_w=ffn_conv_w[l],
        ffn_conv_b=ffn_conv_b[l], w_down=w_down[l].astype(BF16))


def _kv_state(pc, col, lead):
    return pc[:, col:col + N_KV * HEAD_DIM].reshape(*lead, N_KV, HEAD_DIM)


def _prompt_layer(x, mod, p, rel_tiles, nseq, t):
    pc = _in_proj(x, p['g_mix_pre'], mod, p['w_in'], t, tm=1024, tn=512)
    lf, cum, cum_rep = _fox_gates(pc, p['b_fox_row'], nseq, t)
    cum_rows = jnp.transpose(cum[:, :N_HEADS].reshape(nseq, t, N_KV, 2), (0, 2, 3, 1)).reshape(nseq * N_KV, 2, t)
    o_fox = _flash(pc, nseq, t, 'fox', (cum_rep, cum_rows))
    o_sb = _sb_attention(pc, nseq, t)
    gq, gk, gv, gb, gdn_conv = _gdn_prep_prompt(pc, p['gdn_conv_w'], p['alog_row'], p['dt_row'], nseq, t)
    o_gdn, gdn_s = _gdn_chunked(gq, gk, gv, gb, pc, p['gdn_norm_w'], nseq, t)
    scores, thr = _dsa_scores(pc, nseq, t, min(TOPK_MAX, t // 4))
    o_dsa = _flash(pc, nseq, t, 'dsa', (rel_tiles, scores, thr))
    x = _out_proj((o_fox, o_sb, o_gdn, o_dsa), p['w_out'], x, p['g_mix_post'], mod, t, tm=256)
    x, ffn_conv = _ffn_prompt(x, p['g_ffn_pre'], mod, p['w_up'], p['ffn_conv_w'], p['ffn_conv_b'], p['w_down'],
                              p['g_ffn_post'], t, tm=512, tf=512)
    lead = (nseq, t)
    states = (_kv_state(pc, C_FOX_K, lead), _kv_state(pc, C_FOX_V, lead), lf[:, :N_HEADS].reshape(nseq, t, N_HEADS),
              _kv_state(pc, C_SB_K, lead), _kv_state(pc, C_SB_V, lead),
              _kv_state(pc, C_DSA_K, lead), _kv_state(pc, C_DSA_V, lead),
              pc[:, C_DSA_KIDX:C_DSA_KIDX + IDX_DIM].reshape(nseq, t, IDX_DIM), gdn_s, gdn_conv, ffn_conv)
    return x, states


def _heads8(x):
    nb = x.shape[0]
    return jnp.pad(x.reshape(nb, N_HEADS, HEAD_DIM), ((0, 0), (0, SUBLANES - N_HEADS), (0, 0)))


def _sample_layer(layer, x, mod, p, caches, lf_cache8, kidx_cache, page_table, bias_dec, bias0,
                  gdn_s, gdn_conv, ffn_conv):
    nb = x.shape[0]
    pc = _in_proj(x, p['g_mix_pre'], mod, p['w_in'], 1, tm=nb, tn=512)

    def cols(c, n):
        return pc[:, c:c + n]

    gq, gk, gv, gb, lf = _gdn_prep_sample(pc, gdn_conv, p['gdn_conv_w'], p['alog_row'], p['dt_row'], p['b_fox_row'])
    o_gdn, gdn_s_new = _gdn_step(gq, gk, gv, gb, pc, p['gdn_norm_w'], gdn_s)
    gdn_conv_new = jnp.concatenate([gdn_conv[:, 1:], cols(C_GDN_QKV, 3 * GDN_W)[:, None]], axis=1)

    qidx3 = cols(C_DSA_QIDX, IDX_HEADS * IDX_DIM).reshape(nb, IDX_HEADS, IDX_DIM)
    w3 = cols(C_SMALL + S_DSA_W, IDX_HEADS).reshape(nb, IDX_HEADS, 1)
    knew3 = cols(C_DSA_KIDX, IDX_DIM).reshape(nb, 1, IDX_DIM)
    scores3, own3 = _dec_idx_scores(layer, page_table, qidx3, w3, knew3, kidx_cache)
    past = scores3.shape[-1]
    sel, own_sel = _dec_threshold(scores3.reshape(nb, past), own3.reshape(nb, LANES),
                                  min(TOPK_MAX, (past + 1) // 4))

    lf4 = lf[:, :N_HEADS]
    lfn8 = jnp.broadcast_to(jnp.pad(lf4, ((0, 0), (0, SUBLANES - N_HEADS)))[:, :, None], (nb, SUBLANES, LANES))
    q8 = (_heads8(cols(C_FOX_Q, 512)), _heads8(cols(C_SB_Q, 512)), _heads8(cols(C_DSA_Q, 512)))
    new_kv = tuple(cols(c, 256).reshape(nb, N_KV, HEAD_DIM) for c in (C_FOX_K, C_FOX_V, C_DSA_K, C_DSA_V))
    o8 = _dec_attention(layer, page_table, q8, new_kv, lfn8, sel.reshape(nb, 1, N_KV * past),
                        own_sel.reshape(nb, 1, LANES), bias0, bias_dec, caches, lf_cache8)
    o_fox, o_sb, o_dsa = (o[:, :N_HEADS].reshape(nb, N_HEADS * HEAD_DIM) for o in o8)

    x = _out_proj((o_fox, o_sb, o_gdn, o_dsa), p['w_out'], x, p['g_mix_post'], mod, 1, tm=nb)
    x, up = _ffn_sample(x, p['g_ffn_pre'], mod, p['w_up'], p['ffn_conv_w'], p['ffn_conv_b'], p['w_down'],
                        p['g_ffn_post'], ffn_conv, tf=512)
    ffn_conv_new = jnp.concatenate([ffn_conv[:, 1:], up[:, None]], axis=1)
    lead = (nb, 1)
    states = (_kv_state(pc, C_FOX_K, lead), _kv_state(pc, C_FOX_V, lead), lf4.reshape(nb, 1, N_HEADS),
              _kv_state(pc, C_SB_K, lead), _kv_state(pc, C_SB_V, lead),
              _kv_state(pc, C_DSA_K, lead), _kv_state(pc, C_DSA_V, lead),
              cols(C_DSA_KIDX, IDX_DIM).reshape(nb, 1, IDX_DIM), gdn_s_new, gdn_conv_new, ffn_conv_new)
    return x, states


def kernel(x_prompt, x_sample, cache_fox_k, cache_fox_v, cache_fox_logf, cache_sb_k, cache_sb_v, cache_dsa_k,
           cache_dsa_v, cache_dsa_kidx, state_gdn_s, state_gdn_conv, state_ffn_conv, page_table, c_prompt,
           c_sample, w_mod, b_mod, g_mix_pre, g_mix_post, g_ffn_pre, g_ffn_post, w_in, b_fox_f, gdn_a_log,
           gdn_dt_bias, gdn_conv_w, gdn_norm_w, rel_bias, w_out, w_up, ffn_conv_w, ffn_conv_b, w_down):
    nseq, t, d = x_prompt.shape
    nsamp = x_sample.shape[0]
    depth = w_in.shape[0]
    xp = x_prompt.reshape(nseq * t, d)
    xs = x_sample.reshape(nsamp, d)
    pad = (-(nseq + nsamp)) % SUBLANES
    c_all = jnp.concatenate([c_prompt, c_sample, jnp.zeros((pad, d), F32)], axis=0)
    rel_tiles = _rel_bias_tiles(rel_bias, ATTN_TILE)

    pool_shape = cache_fox_k.shape[:3]
    caches = tuple(c.reshape(pool_shape[0], pool_shape[1], pool_shape[2] * N_KV, HEAD_DIM)
                   for c in (cache_fox_k, cache_fox_v, cache_sb_k, cache_sb_v, cache_dsa_k, cache_dsa_v))
    lf_cache8 = jnp.pad(jnp.swapaxes(cache_fox_logf, 2, 3), ((0, 0), (0, 0), (0, SUBLANES - N_HEADS), (0, 0)))
    past = page_table.shape[1] * pool_shape[2]
    rel_dec = past - jnp.arange(N_KV * past, dtype=jnp.int32) // N_KV
    bias_dec = jnp.pad(rel_bias[_t5_bucket(rel_dec)].astype(F32).T, ((0, SUBLANES - N_HEADS), (0, 0)))
    bias0 = jnp.broadcast_to(jnp.pad(rel_bias[0].astype(F32), (0, SUBLANES - N_HEADS))[:, None], (SUBLANES, LANES))

    st_prompt, st_sample = [], []
    for l in range(depth):
        p = _layer_params(l, w_mod, b_mod, g_mix_pre, g_mix_post, g_ffn_pre, g_ffn_post, w_in, b_fox_f, gdn_a_log,
                          gdn_dt_bias, gdn_conv_w, gdn_norm_w, w_out, w_up, ffn_conv_w, ffn_conv_b, w_down)
        mod = _modulation(c_all, p['w_mod'], p['b_mod'])
        xp, sp = _prompt_layer(xp, mod[:nseq], p, rel_tiles, nseq, t)
        xs, ss = _sample_layer(l, xs, mod[nseq:nseq + nsamp], p, caches, lf_cache8, cache_dsa_kidx, page_table,
                               bias_dec, bias0, state_gdn_s[l], state_gdn_conv[l], state_ffn_conv[l])
        st_prompt.append(sp)
        st_sample.append(ss)
    sp = [jnp.stack(z) for z in zip(*st_prompt)]
    ss = [jnp.stack(z) for z in zip(*st_sample)]
    out = [xp.reshape(nseq, t, d), xs.reshape(nsamp, 1, d)]
    for a, b in zip(sp, ss):
        out += [a, b]
    return tuple(out)
```

```python
import functools
import math

import numpy as np
import jax
import jax.numpy as jnp
from jax import lax
from jax.experimental import pallas as pl
from jax.experimental.pallas import tpu as pltpu

F32 = jnp.float32
BF16 = jnp.bfloat16

HEAD_DIM = 128
N_HEADS = 4
N_KV = 2
IDX_HEADS = 16
IDX_DIM = 128
GDN_CONV = 4
GDN_CHUNK = 64
FFN_CONV = 3
TOPK_MAX = 256
N_BUCKETS = 32
MAX_DISTANCE = 128
EPS = 1e-6
NEG_INF = float("-inf")

LANES = 128
SUBLANES = 8
VMEM_LIMIT = 56 * 1024 * 1024
ATTN_TILE = 512

C_FOX_Q, C_FOX_K, C_FOX_V = 0, 512, 768
C_SB_Q, C_SB_K, C_SB_V = 1024, 1536, 1792
C_GDN_QKV, C_GDN_Z = 2048, 3584
C_DSA_QIDX = 4096
C_DSA_Q, C_DSA_K, C_DSA_V = 6144, 6656, 6912
C_DSA_KIDX, C_SMALL = 7168, 7296
D_IN_PAD = 7680
S_FOX_F, S_GDN_A, S_GDN_B, S_DSA_W = 0, 4, 8, 12


def _in_proj_permutation(d_model):
    gdn_qkv = N_HEADS * 3 * HEAD_DIM
    layout = (
        ('fox_q', 512), ('fox_k', 256), ('fox_v', 256), ('fox_f', 4),
        ('sb_q', 512), ('sb_k', 256), ('sb_v', 256),
        ('gdn_qkv', gdn_qkv), ('gdn_a', 4), ('gdn_b', 4), ('gdn_z', 512),
        ('dsa_q', 512), ('dsa_k', 256), ('dsa_v', 256),
        ('dsa_qidx', IDX_HEADS * IDX_DIM), ('dsa_kidx', IDX_DIM), ('dsa_w', IDX_HEADS),
    )
    off = {}
    o = 0
    for name, size in layout:
        off[name] = (o, size)
        o += size
    order = ('fox_q', 'fox_k', 'fox_v', 'sb_q', 'sb_k', 'sb_v', 'gdn_qkv', 'gdn_z', 'dsa_qidx', 'dsa_q', 'dsa_k',
             'dsa_v', 'dsa_kidx', 'fox_f', 'gdn_a', 'gdn_b', 'dsa_w')
    return [off[n] for n in order], o


def _cparams(*sem):
    return pltpu.CompilerParams(dimension_semantics=sem, vmem_limit_bytes=VMEM_LIMIT)


def _rms(x, g):
    return x * lax.rsqrt(jnp.mean(x * x, axis=-1, keepdims=True) + EPS) * g


def _silu(x):
    return x * (1.0 / (1.0 + jnp.exp(-x)))


def _log_sigmoid(x):
    return jnp.minimum(x, 0.0) - jnp.log(1.0 + jnp.exp(-jnp.abs(x)))


def _split3(x):
    hi = x.astype(BF16)
    r1 = x - hi.astype(F32)
    mid = r1.astype(BF16)
    lo = (r1 - mid.astype(F32)).astype(BF16)
    return hi, mid, lo


def _dot(a, b):
    return jnp.dot(a, b, preferred_element_type=F32)


def _dot_nt(a, b):
    return lax.dot_general(a, b, (((1,), (1,)), ((), ())), preferred_element_type=F32)


def _dot_tn(a, b):
    return lax.dot_general(a, b, (((0,), (0,)), ((), ())), preferred_element_type=F32)


def _mod_kernel(c_ref, w_ref, b_ref, o_ref):
    a = _silu(c_ref[...]).astype(BF16)
    o_ref[...] = _dot(a, w_ref[...].astype(BF16)) + b_ref[...]


def _modulation(c, w_mod, b_mod):
    m, d = c.shape
    n = w_mod.shape[1]
    tn = 1024
    return pl.pallas_call(
        _mod_kernel,
        grid=(n // tn,),
        in_specs=[pl.BlockSpec((m, d), lambda j: (0, 0)),
                  pl.BlockSpec((d, tn), lambda j: (0, j)),
                  pl.BlockSpec((1, tn), lambda j: (0, j))],
        out_specs=pl.BlockSpec((m, tn), lambda j: (0, j)),
        out_shape=jax.ShapeDtypeStruct((m, n), F32),
        compiler_params=_cparams("parallel"),
        name="modulation",
    )(c, w_mod, b_mod.reshape(1, n))


def _mod_spec(mod, chunk, tm, rows_per_seq, d):
    if rows_per_seq == 1:
        return mod, pl.BlockSpec((tm, d), lambda i, *_: (i, chunk))
    tiles_per_seq = rows_per_seq // tm
    return (mod.reshape(mod.shape[0], 1, mod.shape[1]),
            pl.BlockSpec((None, 1, d), lambda i, *_: (i // tiles_per_seq, 0, chunk)))


def _in_proj_kernel(x_ref, g_ref, sc_ref, sh_ref, w_ref, o_ref, h_ref):
    @pl.when(pl.program_id(1) == 0)
    def _():
        h = _rms(x_ref[...], g_ref[...]) * (1.0 + sc_ref[...]) + sh_ref[...]
        h_ref[...] = h.astype(BF16)

    o_ref[...] = _dot(h_ref[...], w_ref[...])


def _in_proj(x, g, mod, w, rows_per_seq, tm, tn):
    m, d = x.shape
    n = w.shape[1]
    sc, sc_spec = _mod_spec(mod, 1, tm, rows_per_seq, d)
    sh, sh_spec = _mod_spec(mod, 0, tm, rows_per_seq, d)
    return pl.pallas_call(
        _in_proj_kernel,
        grid=(m // tm, n // tn),
        in_specs=[pl.BlockSpec((tm, d), lambda i, j: (i, 0)),
                  pl.BlockSpec((1, d), lambda i, j: (0, 0)),
                  sc_spec, sh_spec,
                  pl.BlockSpec((d, tn), lambda i, j: (0, j))],
        out_specs=pl.BlockSpec((tm, tn), lambda i, j: (i, j)),
        out_shape=jax.ShapeDtypeStruct((m, n), F32),
        scratch_shapes=[pltpu.VMEM((tm, d), BF16)],
        compiler_params=_cparams("parallel", "arbitrary"),
        name="in_proj",
    )(x, g.reshape(1, d), sc, sh, w)


def _out_proj_kernel(a0_ref, a1_ref, a2_ref, a3_ref, w_ref, x_ref, g_ref, gate_ref, o_ref):
    m = None
    for n, a_ref in enumerate((a0_ref, a1_ref, a2_ref, a3_ref)):
        k = a_ref.shape[1]
        part = _dot(a_ref[...].astype(BF16), w_ref[n * k:(n + 1) * k, :])
        m = part if m is None else m + part
    o_ref[...] = x_ref[...] + gate_ref[...] * _rms(m, g_ref[...])


def _out_proj(mixed, w, x, g, mod, rows_per_seq, tm):
    m, k4 = mixed[0].shape
    k, d = w.shape
    gate, gate_spec = _mod_spec(mod, 2, tm, rows_per_seq, d)
    a_spec = pl.BlockSpec((tm, k4), lambda i: (i, 0))
    return pl.pallas_call(
        _out_proj_kernel,
        grid=(m // tm,),
        in_specs=[a_spec, a_spec, a_spec, a_spec,
                  pl.BlockSpec((k, d), lambda i: (0, 0)),
                  pl.BlockSpec((tm, d), lambda i: (i, 0)),
                  pl.BlockSpec((1, d), lambda i: (0, 0)),
                  gate_spec],
        out_specs=pl.BlockSpec((tm, d), lambda i: (i, 0)),
        out_shape=jax.ShapeDtypeStruct((m, d), F32),
        compiler_params=_cparams("parallel"),
        name="out_proj",
    )(*mixed, w, x, g.reshape(1, d), gate)


def _shift_rows(u, prev, k):
    rolled = pltpu.roll(u, k, axis=0)
    row = lax.broadcasted_iota(jnp.int32, u.shape, 0)
    out = rolled
    for r in range(k):
        out = jnp.where(row == r, prev[SUBLANES - k + r:SUBLANES - k + r + 1, :], out)
    return out


def _ffn_prompt_kernel(x_ref, g_ref, sc_ref, sh_ref, wg_ref, wv_ref, cwg_ref, cwv_ref, cbg_ref, cbv_ref,
                       wd_ref, gpost_ref, gate_ref, o_ref, sg_ref, sv_ref,
                       h_ref, acc_ref, cg_ref, cv_ref, *, tiles_per_seq):
    i = pl.program_id(0)
    j = pl.program_id(1)

    @pl.when(j == 0)
    def _():
        h = _rms(x_ref[...], g_ref[...]) * (1.0 + sc_ref[...]) + sh_ref[...]
        h_ref[...] = h.astype(BF16)
        acc_ref[...] = jnp.zeros_like(acc_ref)

    fresh = i % tiles_per_seq == 0

    def conv(u, carry_ref, cw_ref, cb_ref):
        prev = jnp.where(fresh, 0.0, carry_ref[j])
        out = cw_ref[0:1, :] * _shift_rows(u, prev, 2) + cw_ref[1:2, :] * _shift_rows(u, prev, 1)
        out = out + cw_ref[2:3, :] * u + cb_ref[...]
        carry_ref[j] = u[u.shape[0] - SUBLANES:, :]
        return out

    ug = _dot(h_ref[...], wg_ref[...])
    uv = _dot(h_ref[...], wv_ref[...])
    sg_ref[...] = ug[ug.shape[0] - SUBLANES:, :]
    sv_ref[...] = uv[uv.shape[0] - SUBLANES:, :]
    gate = conv(ug, cg_ref, cwg_ref, cbg_ref)
    val = conv(uv, cv_ref, cwv_ref, cbv_ref)
    hid = (_silu(gate) * val).astype(BF16)
    acc_ref[...] += _dot(hid, wd_ref[...])

    @pl.when(j == pl.num_programs(1) - 1)
    def _():
        o_ref[...] = x_ref[...] + gate_ref[...] * _rms(acc_ref[...], gpost_ref[...])


def _ffn_prompt(x, g_pre, mod, w_up, conv_w, conv_b, w_down, g_post, rows_per_seq, tm, tf):
    m, d = x.shape
    f = w_down.shape[0]
    nf = f // tf
    nseq = m // rows_per_seq
    tiles_per_seq = rows_per_seq // tm
    sc, sc_spec = _mod_spec(mod, 4, tm, rows_per_seq, d)
    sh, sh_spec = _mod_spec(mod, 3, tm, rows_per_seq, d)
    gate, gate_spec = _mod_spec(mod, 5, tm, rows_per_seq, d)
    state_spec = pl.BlockSpec((None, SUBLANES, tf), lambda i, j: (i, 0, j))
    out, sg, sv = pl.pallas_call(
        functools.partial(_ffn_prompt_kernel, tiles_per_seq=tiles_per_seq),
        grid=(m // tm, nf),
        in_specs=[pl.BlockSpec((tm, d), lambda i, j: (i, 0)),
                  pl.BlockSpec((1, d), lambda i, j: (0, 0)),
                  sc_spec, sh_spec,
                  pl.BlockSpec((d, tf), lambda i, j: (0, j)),
                  pl.BlockSpec((d, tf), lambda i, j: (0, j + nf)),
                  pl.BlockSpec((FFN_CONV, tf), lambda i, j: (0, j)),
                  pl.BlockSpec((FFN_CONV, tf), lambda i, j: (0, j + nf)),
                  pl.BlockSpec((1, tf), lambda i, j: (0, j)),
                  pl.BlockSpec((1, tf), lambda i, j: (0, j + nf)),
                  pl.BlockSpec((tf, d), lambda i, j: (j, 0)),
                  pl.BlockSpec((1, d), lambda i, j: (0, 0)),
                  gate_spec],
        out_specs=[pl.BlockSpec((tm, d), lambda i, j: (i, 0)), state_spec, state_spec],
        out_shape=[jax.ShapeDtypeStruct((m, d), F32),
                   jax.ShapeDtypeStruct((m // tm, SUBLANES, f), F32),
                   jax.ShapeDtypeStruct((m // tm, SUBLANES, f), F32)],
        scratch_shapes=[pltpu.VMEM((tm, d), BF16), pltpu.VMEM((tm, d), F32),
                        pltpu.VMEM((nf, SUBLANES, tf), F32), pltpu.VMEM((nf, SUBLANES, tf), F32)],
        compiler_params=_cparams("arbitrary", "arbitrary"),
        name="ffn_prompt",
    )(x, g_pre.reshape(1, d), sc, sh, w_up, w_up, conv_w, conv_w, conv_b.reshape(1, 2 * f),
      conv_b.reshape(1, 2 * f), w_down, g_post.reshape(1, d), gate)
    state = jnp.concatenate([sg, sv], axis=-1)[tiles_per_seq - 1::tiles_per_seq, SUBLANES - (FFN_CONV - 1):, :]
    return out, state


def _ffn_sample_kernel(x_ref, g_ref, sc_ref, sh_ref, wg_ref, wv_ref, cwg_ref, cwv_ref, cbg_ref, cbv_ref,
                       p0g_ref, p0v_ref, p1g_ref, p1v_ref, wd_ref, gpost_ref, gate_ref,
                       o_ref, ug_ref, uv_ref, h_ref, acc_ref):
    j = pl.program_id(1)

    @pl.when(j == 0)
    def _():
        h = _rms(x_ref[...], g_ref[...]) * (1.0 + sc_ref[...]) + sh_ref[...]
        h_ref[...] = h.astype(BF16)
        acc_ref[...] = jnp.zeros_like(acc_ref)

    ug = _dot(h_ref[...], wg_ref[...])
    uv = _dot(h_ref[...], wv_ref[...])
    ug_ref[...] = ug
    uv_ref[...] = uv
    gate = cwg_ref[0:1, :] * p0g_ref[...] + cwg_ref[1:2, :] * p1g_ref[...] + cwg_ref[2:3, :] * ug + cbg_ref[...]
    val = cwv_ref[0:1, :] * p0v_ref[...] + cwv_ref[1:2, :] * p1v_ref[...] + cwv_ref[2:3, :] * uv + cbv_ref[...]
    hid = (_silu(gate) * val).astype(BF16)
    acc_ref[...] += _dot(hid, wd_ref[...])

    @pl.when(j == pl.num_programs(1) - 1)
    def _():
        o_ref[...] = x_ref[...] + gate_ref[...] * _rms(acc_ref[...], gpost_ref[...])


def _ffn_sample(x, g_pre, mod, w_up, conv_w, conv_b, w_down, g_post, prev, tf):
    m, d = x.shape
    f = w_down.shape[0]
    nf = f // tf
    tm = m
    sc, sc_spec = _mod_spec(mod, 4, tm, 1, d)
    sh, sh_spec = _mod_spec(mod, 3, tm, 1, d)
    gate, gate_spec = _mod_spec(mod, 5, tm, 1, d)
    prev2 = prev.reshape(m, (FFN_CONV - 1) * 2 * f)
    up_spec = pl.BlockSpec((tm, tf), lambda i, j: (i, j))
    out, up_g, up_v = pl.pallas_call(
        _ffn_sample_kernel,
        grid=(1, nf),
        in_specs=[pl.BlockSpec((tm, d), lambda i, j: (i, 0)),
                  pl.BlockSpec((1, d), lambda i, j: (0, 0)),
                  sc_spec, sh_spec,
                  pl.BlockSpec((d, tf), lambda i, j: (0, j)),
                  pl.BlockSpec((d, tf), lambda i, j: (0, j + nf)),
                  pl.BlockSpec((FFN_CONV, tf), lambda i, j: (0, j)),
                  pl.BlockSpec((FFN_CONV, tf), lambda i, j: (0, j + nf)),
                  pl.BlockSpec((1, tf), lambda i, j: (0, j)),
                  pl.BlockSpec((1, tf), lambda i, j: (0, j + nf)),
                  pl.BlockSpec((tm, tf), lambda i, j: (i, j)),
                  pl.BlockSpec((tm, tf), lambda i, j: (i, j + nf)),
                  pl.BlockSpec((tm, tf), lambda i, j: (i, j + 2 * nf)),
                  pl.BlockSpec((tm, tf), lambda i, j: (i, j + 3 * nf)),
                  pl.BlockSpec((tf, d), lambda i, j: (j, 0)),
                  pl.BlockSpec((1, d), lambda i, j: (0, 0)),
                  gate_spec],
        out_specs=[pl.BlockSpec((tm, d), lambda i, j: (i, 0)), up_spec, up_spec],
        out_shape=[jax.ShapeDtypeStruct((m, d), F32), jax.ShapeDtypeStruct((m, f), F32),
                   jax.ShapeDtypeStruct((m, f), F32)],
        scratch_shapes=[pltpu.VMEM((tm, d), BF16), pltpu.VMEM((tm, d), F32)],
        compiler_params=_cparams("arbitrary", "arbitrary"),
        name="ffn_sample",
    )(x, g_pre.reshape(1, d), sc, sh, w_up, w_up, conv_w, conv_w, conv_b.reshape(1, 2 * f),
      conv_b.reshape(1, 2 * f), prev2, prev2, prev2, prev2, w_down, g_post.reshape(1, d), gate)
    return out, jnp.concatenate([up_g, up_v], axis=-1)


def _tri(n, fn):
    r = lax.broadcasted_iota(jnp.int32, (n, n), 0)
    c = lax.broadcasted_iota(jnp.int32, (n, n), 1)
    return jnp.where(fn(r, c), 1.0, 0.0).astype(BF16)


def _dot_exact01(m01, x):
    hi, mid, lo = _split3(x)
    return _dot(m01, hi) + _dot(m01, mid) + _dot(m01, lo)


def _fox_gate_kernel(s_ref, b_ref, lf_ref, cum_ref, rep_ref, carry_ref):
    @pl.when(pl.program_id(1) == 0)
    def _():
        carry_ref[...] = jnp.zeros_like(carry_ref)

    lf = _log_sigmoid(s_ref[...] + b_ref[...])
    n = lf.shape[0]
    cum = _dot_exact01(_tri(n, lambda r, c: r >= c), lf) + carry_ref[0:1, :]
    lf_ref[...] = lf
    cum_ref[...] = cum
    for h in range(N_HEADS):
        rep_ref[:, h * LANES:(h + 1) * LANES] = jnp.broadcast_to(cum[:, S_FOX_F + h:S_FOX_F + h + 1], (n, LANES))
    carry_ref[...] = jnp.broadcast_to(cum[n - 1:n, :], carry_ref.shape)


def _fox_gates(pc, b_row, nseq, t, tc=256):
    nt = t // tc
    cb = C_SMALL // LANES
    spec = pl.BlockSpec((tc, LANES), lambda b, i: (b * nt + i, 0))
    return pl.pallas_call(
        _fox_gate_kernel,
        grid=(nseq, nt),
        in_specs=[pl.BlockSpec((tc, LANES), lambda b, i: (b * nt + i, cb)),
                  pl.BlockSpec((1, LANES), lambda b, i: (0, 0))],
        out_specs=[spec, spec, pl.BlockSpec((tc, N_HEADS * LANES), lambda b, i: (b * nt + i, 0))],
        out_shape=[jax.ShapeDtypeStruct((nseq * t, LANES), F32)] * 2
        + [jax.ShapeDtypeStruct((nseq * t, N_HEADS * LANES), F32)],
        scratch_shapes=[pltpu.VMEM((SUBLANES, LANES), F32)],
        compiler_params=_cparams("parallel", "arbitrary"),
        name="fox_gates",
    )(pc, b_row)


MASKED = -1e30


def _flash_kernel(*refs, mode, tq, tk):
    if mode == 'fox':
        q_ref, k_ref, v_ref, cq_ref, ck_ref, o_ref, m_ref, acc_ref = refs
    else:
        q_ref, k_ref, v_ref, bias_ref, sc_ref, thr_ref, o_ref, m_ref, acc_ref = refs
    qi = pl.program_id(2)
    ki = pl.program_id(3)

    @pl.when(ki == 0)
    def _():
        m_ref[...] = jnp.full_like(m_ref, MASKED)
        acc_ref[...] = jnp.zeros_like(acc_ref)

    def step(diagonal):
        t_pos = qi * tq + lax.broadcasted_iota(jnp.int32, (tq, LANES), 0)
        lane = lax.broadcasted_iota(jnp.int32, (tq, LANES), 1)
        masks = []
        for c in range(tk // LANES):
            mk = ((ki * tk + c * LANES + lane) <= t_pos) if diagonal else None
            if mode == 'dsa':
                sel = sc_ref[:, c * LANES:(c + 1) * LANES] >= thr_ref[...]
                mk = sel if mk is None else mk & sel
            masks.append(mk)
        kb = k_ref[...].astype(BF16)
        v_ext = jnp.concatenate([v_ref[...].astype(BF16), jnp.ones((tk, LANES), BF16)], axis=1)
        for g in range(2):
            q = q_ref[:, g * HEAD_DIM:(g + 1) * HEAD_DIM].astype(BF16)
            s = _dot_nt(q, kb) * HEAD_DIM ** -0.5
            blocks = []
            for c, mk in enumerate(masks):
                blk = s[:, c * LANES:(c + 1) * LANES]
                if mode == 'fox':
                    blk = blk + (cq_ref[:, g * LANES:(g + 1) * LANES] - ck_ref[g:g + 1, c * LANES:(c + 1) * LANES])
                else:
                    blk = blk + bias_ref[g, :, c * LANES:(c + 1) * LANES]
                blocks.append(blk if mk is None else jnp.where(mk, blk, MASKED))
            top = functools.reduce(jnp.maximum, blocks)
            m_old = m_ref[g]
            m_new = jnp.maximum(m_old, jnp.max(top, axis=-1, keepdims=True))
            alpha = jnp.exp(m_old - m_new)
            p = jnp.concatenate([jnp.exp(blk - m_new) if mk is None else jnp.where(mk, jnp.exp(blk - m_new), 0.0)
                                 for blk, mk in zip(blocks, masks)], axis=1)
            pv = _dot(p.astype(BF16), v_ext)
            acc_ref[g, :, 0:LANES] = alpha * acc_ref[g, :, 0:LANES] + pv[:, 0:LANES]
            acc_ref[g, :, LANES:2 * LANES] = alpha * acc_ref[g, :, LANES:2 * LANES] + pv[:, LANES:2 * LANES]
            m_ref[g] = m_new

    @pl.when(ki < qi)
    def _():
        step(False)

    @pl.when(ki == qi)
    def _():
        step(True)
        for g in range(2):
            o_ref[:, g * HEAD_DIM:(g + 1) * HEAD_DIM] = acc_ref[g, :, 0:LANES] / acc_ref[g, :, LANES:2 * LANES]


def _flash(pc, nseq, t, mode, extra, tq=ATTN_TILE):
    tk = tq
    nq = t // tq
    c_q, c_k, c_v = (C_FOX_Q, C_FOX_K, C_FOX_V) if mode == 'fox' else (C_DSA_Q, C_DSA_K, C_DSA_V)
    qb, kb, vb = c_q // 256, c_k // HEAD_DIM, c_v // HEAD_DIM
    in_specs = [pl.BlockSpec((tq, 256), lambda b, h, i, j: (b * nq + i, qb + h)),
                pl.BlockSpec((tk, HEAD_DIM), lambda b, h, i, j: (b * nq + jnp.minimum(i, j), kb + h)),
                pl.BlockSpec((tk, HEAD_DIM), lambda b, h, i, j: (b * nq + jnp.minimum(i, j), vb + h))]
    if mode == 'fox':
        in_specs += [pl.BlockSpec((tq, 2 * LANES), lambda b, h, i, j: (b * nq + i, h)),
                     pl.BlockSpec((None, 2, tk), lambda b, h, i, j: (b * N_KV + h, 0, jnp.minimum(i, j)))]
    else:
        in_specs += [pl.BlockSpec((None, 2, None, tq, tk),
                                  lambda b, h, i, j: (h, 0, jnp.clip(i - j, 0, 2), 0, 0)),
                     pl.BlockSpec((tq, tk), lambda b, h, i, j: (b * nq + i, jnp.minimum(i, j))),
                     pl.BlockSpec((tq, LANES), lambda b, h, i, j: (b * nq + i, 0))]
    return pl.pallas_call(
        functools.partial(_flash_kernel, mode=mode, tq=tq, tk=tk),
        grid=(nseq, N_KV, nq, nq),
        in_specs=in_specs,
        out_specs=pl.BlockSpec((tq, 256), lambda b, h, i, j: (b * nq + i, h)),
        out_shape=jax.ShapeDtypeStruct((nseq * t, N_HEADS * HEAD_DIM), F32),
        scratch_shapes=[pltpu.VMEM((2, tq, LANES), F32), pltpu.VMEM((2, tq, 2 * LANES), F32)],
        compiler_params=_cparams("parallel", "parallel", "parallel", "arbitrary"),
        name="flash_" + mode,
    )(pc, pc, pc, *extra)


def _split2(x):
    hi = x.astype(BF16)
    return hi, (x - hi.astype(F32)).astype(BF16)


def _sb_kernel(q_ref, k_ref, v_ref, o_ref, r_ref, acc_ref, *, tq, tk):
    qi = pl.program_id(2)
    ki = pl.program_id(3)

    @pl.when(ki == 0)
    def _():
        r_ref[...] = jnp.zeros_like(r_ref)
        acc_ref[...] = jnp.zeros_like(acc_ref)

    def step(diagonal):
        if diagonal:
            mask = (lax.broadcasted_iota(jnp.int32, (tq, tk), 1) < lax.broadcasted_iota(jnp.int32, (tq, tk), 0))
        after = jnp.concatenate([_tri(LANES, lambda r, c: r > c), jnp.ones((LANES, LANES), BF16)], axis=1)
        kb = k_ref[...].astype(BF16)
        vb = v_ref[...].astype(BF16)
        nblk = tk // LANES
        for g in range(2):
            q = q_ref[:, g * HEAD_DIM:(g + 1) * HEAD_DIM].astype(BF16)
            z = _dot_nt(q, kb) * HEAD_DIM ** -0.5
            sp = jnp.log(1.0 + jnp.exp(-jnp.abs(z)))
            log_take = jnp.minimum(z, 0.0) - sp
            log_stay = jnp.minimum(-z, 0.0) - sp
            if diagonal:
                log_stay = jnp.where(mask, log_stay, 0.0)
            carry = r_ref[g]
            later = [None] * nblk
            for c in reversed(range(nblk)):
                hi, lo = _split2(log_stay[:, c * LANES:(c + 1) * LANES])
                sums = _dot(hi, after) + _dot(lo, after)
                later[c] = sums[:, 0:LANES] + carry
                carry = carry + sums[:, LANES:2 * LANES]
            a = jnp.exp(log_take + jnp.concatenate(later, axis=1))
            if diagonal:
                a = jnp.where(mask, a, 0.0)
            acc_ref[g] += _dot(a.astype(BF16), vb)
            r_ref[g] = carry

    @pl.when(ki == 0)
    def _():
        step(True)

    @pl.when((ki > 0) & (ki <= qi))
    def _():
        step(False)

    @pl.when(ki == qi)
    def _():
        for g in range(2):
            o_ref[:, g * HEAD_DIM:(g + 1) * HEAD_DIM] = acc_ref[g]


def _sb_attention(pc, nseq, t, tq=ATTN_TILE):
    tk = tq
    nq = t // tq
    qb, kb, vb = C_SB_Q // 256, C_SB_K // HEAD_DIM, C_SB_V // HEAD_DIM
    return pl.pallas_call(
        functools.partial(_sb_kernel, tq=tq, tk=tk),
        grid=(nseq, N_KV, nq, nq),
        in_specs=[pl.BlockSpec((tq, 256), lambda b, h, i, j: (b * nq + i, qb + h)),
                  pl.BlockSpec((tk, HEAD_DIM), lambda b, h, i, j: (b * nq + jnp.maximum(i - j, 0), kb + h)),
                  pl.BlockSpec((tk, HEAD_DIM), lambda b, h, i, j: (b * nq + jnp.maximum(i - j, 0), vb + h))],
        out_specs=pl.BlockSpec((tq, 256), lambda b, h, i, j: (b * nq + i, h)),
        out_shape=jax.ShapeDtypeStruct((nseq * t, N_HEADS * HEAD_DIM), F32),
        scratch_shapes=[pltpu.VMEM((2, tq, LANES), F32), pltpu.VMEM((2, tq, HEAD_DIM), F32)],
        compiler_params=_cparams("parallel", "parallel", "parallel", "arbitrary"),
        name="sb_attention",
    )(pc, pc, pc)


KEY_NEG_INF = -2139095041


def _float_key(x):
    b = pltpu.bitcast(x, jnp.int32)
    return b ^ ((b >> 31) & 0x7FFFFFFF)


def _key_float(k):
    return pltpu.bitcast(k ^ ((k >> 31) & 0x7FFFFFFF), F32)


def _kth_largest_key(count_ge, rows, k):
    def body(i, res):
        bit = 31 - i
        cand = jnp.where(bit == 31, jnp.zeros_like(res), res | jnp.left_shift(1, jnp.minimum(bit, 30)))
        return jnp.where(count_ge(cand) >= k, cand, res)

    res = jnp.full((rows, 1), jnp.iinfo(jnp.int32).min, jnp.int32)
    return lax.fori_loop(0, 32, body, res)


def _tie_cut(tied_before, rows, need, nbits):
    def body(i, c):
        cand = c | jnp.left_shift(1, nbits - i)
        return jnp.where(tied_before(cand) < need, cand, c)

    return lax.fori_loop(0, nbits + 1, body, jnp.zeros((rows, 1), jnp.int32))


def _dsa_score_kernel(qi_ref, ki_ref, w_ref, sc_ref, thr_ref, key_ref, wrep_ref, *, tq, tk, topk):
    qi = pl.program_id(1)
    ki = pl.program_id(2)

    @pl.when((qi == 0) & (ki == 0))
    def _():
        key_ref[...] = jnp.full(key_ref.shape, jnp.iinfo(jnp.int32).min, jnp.int32)

    @pl.when(ki == 0)
    def _():
        hi, mid, lo = _split3(w_ref[...] * (IDX_DIM ** -0.5 * IDX_HEADS ** -0.5))
        r = lax.broadcasted_iota(jnp.int32, (LANES, LANES), 0)
        for h in range(IDX_HEADS):
            pick = jnp.where(r == S_DSA_W + h, 1.0, 0.0).astype(BF16)
            wrep_ref[h] = _dot(hi, pick) + _dot(mid, pick) + _dot(lo, pick)

    @pl.when(ki <= qi)
    def _():
        kidx = ki_ref[...].astype(BF16)
        nblk = tk // LANES
        blocks = [jnp.zeros((tq, LANES), F32)] * nblk
        for h in range(IDX_HEADS):
            q = qi_ref[:, h * IDX_DIM:(h + 1) * IDX_DIM].astype(BF16)
            si = _dot_nt(q, kidx)
            w = wrep_ref[h]
            blocks = [blk + jnp.maximum(si[:, c * LANES:(c + 1) * LANES], 0.0) * w for c, blk in enumerate(blocks)]
        score = jnp.concatenate(blocks, axis=1)
        t_pos = qi * tq + lax.broadcasted_iota(jnp.int32, (tq, tk), 0)
        s_pos = ki * tk + lax.broadcasted_iota(jnp.int32, (tq, tk), 1)
        score = jnp.where(s_pos <= t_pos, score, NEG_INF)
        key_ref[:, pl.ds(pl.multiple_of(ki * tk, tk), tk)] = _float_key(score)

    def search(width):
        def count_ge(cand):
            parts = []
            for r0 in range(0, tq, LANES):
                cand_r = cand[r0:r0 + LANES, :]
                cnt = jnp.zeros((LANES, LANES), jnp.int32)
                for c0 in range(0, width, LANES):
                    cnt = cnt + jnp.where(key_ref[r0:r0 + LANES, c0:c0 + LANES] >= cand_r, 1, 0)
                parts.append(jnp.sum(cnt, axis=-1, keepdims=True))
            return jnp.concatenate(parts, axis=0)

        res = jnp.maximum(_kth_largest_key(count_ge, tq, topk), KEY_NEG_INF)
        thr_ref[...] = jnp.broadcast_to(_key_float(res), thr_ref.shape)

        cnt_gt = count_ge(res + 1)
        need = topk - cnt_gt
        surplus = ((count_ge(res) - cnt_gt) > need) & (res > KEY_NEG_INF)

        @pl.when(jnp.max(jnp.where(surplus, 1, 0)) > 0)
        def _():
            pos = lax.broadcasted_iota(jnp.int32, (tq, width), 1)
            tied = key_ref[:, 0:width] == res

            def tied_before(c):
                return jnp.sum(jnp.where(tied & (pos < c), 1, 0), axis=-1, keepdims=True)

            cut = _tie_cut(tied_before, tq, jnp.maximum(need, 1), width.bit_length())
            drop = tied & (pos > cut) & surplus
            key_ref[:, 0:width] = jnp.where(drop, res - 1, key_ref[:, 0:width])

    total = key_ref.shape[1]
    widths = sorted({max(tk, (total * n // 4) // tk * tk) for n in (1, 2, 3, 4)})
    for n, width in enumerate(widths):
        lo = widths[n - 1] if n else 0

        @pl.when((ki == qi) & ((qi + 1) * tk > lo) & ((qi + 1) * tk <= width))
        def _(width=width):
            search(width)

    @pl.when(ki == qi)
    def _():
        keys = key_ref[...]
        sc_ref[...] = jnp.where(keys < KEY_NEG_INF, NEG_INF, _key_float(keys))


def _dsa_scores(pc, nseq, t, topk, tq=256):
    tk = tq
    nq = t // tq
    return pl.pallas_call(
        functools.partial(_dsa_score_kernel, tq=tq, tk=tk, topk=topk),
        grid=(nseq, nq, nq),
        in_specs=[pl.BlockSpec((tq, IDX_HEADS * IDX_DIM), lambda b, i, j: (b * nq + i, C_DSA_QIDX // 2048)),
                  pl.BlockSpec((tk, IDX_DIM), lambda b, i, j: (b * nq + jnp.minimum(i, j), C_DSA_KIDX // IDX_DIM)),
                  pl.BlockSpec((tq, LANES), lambda b, i, j: (b * nq + i, C_SMALL // LANES))],
        out_specs=[pl.BlockSpec((tq, t), lambda b, i, j: (b * nq + i, 0)),
                   pl.BlockSpec((tq, LANES), lambda b, i, j: (b * nq + i, 0))],
        out_shape=[jax.ShapeDtypeStruct((nseq * t, t), F32), jax.ShapeDtypeStruct((nseq * t, LANES), F32)],
        scratch_shapes=[pltpu.VMEM((tq, t), jnp.int32), pltpu.VMEM((IDX_HEADS, tq, LANES), F32)],
        compiler_params=_cparams("arbitrary", "arbitrary", "arbitrary"),
        name="dsa_scores",
    )(pc, pc, pc)


def _t5_bucket(rel):
    n = jnp.maximum(rel, 0)
    max_exact = N_BUCKETS // 2
    nf = jnp.maximum(n, 1).astype(F32)
    large = max_exact + (jnp.log(nf / max_exact) / math.log(MAX_DISTANCE / max_exact)
                         * (N_BUCKETS - max_exact)).astype(jnp.int32)
    large = jnp.minimum(large, N_BUCKETS - 1)
    return jnp.where(n < max_exact, n, large)


def _rel_bias_tiles(rel_bias, tq):
    i = jnp.arange(tq, dtype=jnp.int32)[:, None]
    j = jnp.arange(tq, dtype=jnp.int32)[None, :]
    rel = jnp.stack([i - j, tq + i - j, 2 * tq + i - j])
    return _bucket_bias(rel_bias, _t5_bucket(rel)).reshape(N_KV, 2, 3, tq, tq)


def _bucket_bias(rel_bias, bucket):
    table = rel_bias.astype(F32)
    out = jnp.zeros((table.shape[1],) + bucket.shape, F32)
    for b in range(table.shape[0]):
        out = jnp.where(bucket[None] == b, table[b].reshape((-1,) + (1,) * bucket.ndim), out)
    return out


GDN_W = N_HEADS * HEAD_DIM


def _softplus(x):
    return jnp.maximum(x, 0.0) + jnp.log(1.0 + jnp.exp(-jnp.abs(x)))


def _l2norm_heads(x, scale):
    parts = []
    for h in range(N_HEADS):
        xh = x[:, h * HEAD_DIM:(h + 1) * HEAD_DIM]
        parts.append(xh * (lax.rsqrt(jnp.sum(xh * xh, axis=-1, keepdims=True) + EPS) * scale))
    return jnp.concatenate(parts, axis=-1)


def _gdn_gates(small, alog_row, dt_row):
    glog = -jnp.exp(alog_row) * _softplus(small + dt_row)
    beta = 1.0 / (1.0 + jnp.exp(-small))
    lane = lax.broadcasted_iota(jnp.int32, (small.shape[0], LANES), 1)
    blocks = []
    for h in range(N_HEADS):
        g_col = glog[:, S_GDN_A + h:S_GDN_A + h + 1]
        b_col = beta[:, S_GDN_B + h:S_GDN_B + h + 1]
        blocks.append(jnp.where(lane == 0, g_col, jnp.where(lane == 1, b_col, 0.0)))
    return jnp.concatenate(blocks, axis=-1)


def _gdn_prep_prompt_kernel(uq_ref, uk_ref, uv_ref, cw_ref, small_ref, alog_ref, dt_ref,
                            q_ref, k_ref, v_ref, gb_ref, sq_ref, sk_ref, sv_ref, carry_ref):
    @pl.when(pl.program_id(1) == 0)
    def _():
        carry_ref[...] = jnp.zeros_like(carry_ref)

    def conv(idx, u_ref, s_ref):
        u = u_ref[...]
        prev = carry_ref[idx]
        cw = cw_ref[:, idx * GDN_W:(idx + 1) * GDN_W]
        out = cw[GDN_CONV - 1:GDN_CONV, :] * u
        for j in range(GDN_CONV - 1):
            out = out + cw[j:j + 1, :] * _shift_rows(u, prev, GDN_CONV - 1 - j)
        last = u[u.shape[0] - SUBLANES:, :]
        carry_ref[idx] = last
        s_ref[...] = last
        return _silu(out)

    q_ref[...] = _l2norm_heads(conv(0, uq_ref, sq_ref), HEAD_DIM ** -0.5)
    k_ref[...] = _l2norm_heads(conv(1, uk_ref, sk_ref), 1.0)
    v_ref[...] = conv(2, uv_ref, sv_ref)
    gb_ref[...] = _gdn_gates(small_ref[...], alog_ref[...], dt_ref[...])


def _gdn_prep_prompt(pc, conv_w, alog_row, dt_row, nseq, t, tm=256):
    nt = t // tm
    ub = C_GDN_QKV // GDN_W
    row_spec = pl.BlockSpec((tm, GDN_W), lambda b, i: (b * nt + i, 0))
    st_spec = pl.BlockSpec((None, SUBLANES, GDN_W), lambda b, i: (b, 0, 0))
    par_spec = pl.BlockSpec((1, LANES), lambda b, i: (0, 0))
    outs = pl.pallas_call(
        _gdn_prep_prompt_kernel,
        grid=(nseq, nt),
        in_specs=[pl.BlockSpec((tm, GDN_W), lambda b, i: (b * nt + i, ub)),
                  pl.BlockSpec((tm, GDN_W), lambda b, i: (b * nt + i, ub + 1)),
                  pl.BlockSpec((tm, GDN_W), lambda b, i: (b * nt + i, ub + 2)),
                  pl.BlockSpec((GDN_CONV, 3 * GDN_W), lambda b, i: (0, 0)),
                  pl.BlockSpec((tm, LANES), lambda b, i: (b * nt + i, C_SMALL // LANES)),
                  par_spec, par_spec],
        out_specs=[row_spec, row_spec, row_spec, row_spec, st_spec, st_spec, st_spec],
        out_shape=[jax.ShapeDtypeStruct((nseq * t, GDN_W), F32)] * 4
        + [jax.ShapeDtypeStruct((nseq, SUBLANES, GDN_W), F32)] * 3,
        scratch_shapes=[pltpu.VMEM((3, SUBLANES, GDN_W), F32)],
        compiler_params=_cparams("parallel", "arbitrary"),
        name="gdn_prep_prompt",
    )(pc, pc, pc, conv_w, pc, alog_row, dt_row)
    q, k, v, gb, sq, sk, sv = outs
    conv_state = jnp.concatenate([sq, sk, sv], axis=-1)[:, SUBLANES - (GDN_CONV - 1):, :]
    return q, k, v, gb, conv_state


def _dot_3pass(a, b):
    a_hi, a_lo = _split2(a)
    b_hi, b_lo = _split2(b)
    return _dot(a_hi, b_hi) + (_dot(a_hi, b_lo) + _dot(a_lo, b_hi))


def _gdn_chunk_kernel(q_ref, k_ref, v_ref, gb_ref, z_ref, nw_ref, o_ref, sfin_ref, s_ref, *, chunks, heads):
    c = GDN_CHUNK

    @pl.when(pl.program_id(2) == 0)
    def _():
        s_ref[...] = jnp.zeros_like(s_ref)

    n = chunks * c
    r = lax.broadcasted_iota(jnp.int32, (n, n), 0)
    col = lax.broadcasted_iota(jnp.int32, (n, n), 1)
    same = (r // c) == (col // c)
    lower = same & (r >= col)
    strict = same & (r > col)
    eye = jnp.where(r == col, 1.0, 0.0)
    low01 = jnp.where(lower, 1.0, 0.0).astype(BF16)
    up01 = jnp.where(same & (col > r), 1.0, 0.0).astype(BF16)

    def chunk_terms(hh):
        cols = slice(hh * HEAD_DIM, (hh + 1) * HEAD_DIM)
        q = q_ref[:, cols]
        k = k_ref[:, cols]
        v = v_ref[:, cols]
        g = gb_ref[:, hh * HEAD_DIM:hh * HEAD_DIM + 1]
        beta = gb_ref[:, hh * HEAD_DIM + 1:hh * HEAD_DIM + 2]
        g_lanes = jnp.broadcast_to(g, (n, HEAD_DIM))
        gc = _dot_exact01(low01, g_lanes)
        rest = _dot_exact01(up01, g_lanes)
        diff = _dot_exact01(low01, jnp.where(strict, jnp.broadcast_to(g, (n, n)), 0.0))
        decay = jnp.where(lower, jnp.exp(jnp.where(lower, diff, 0.0)), 0.0)
        kb = k * beta
        lmat = jnp.where(strict, _dot_nt(kb.astype(BF16), k.astype(BF16)) * decay, 0.0)
        p = -lmat
        tinv = eye + p
        for _ in range(5):
            p = _dot_3pass(p, p)
            tinv = tinv + _dot_3pass(tinv, p)
        tb = tinv.astype(BF16)
        u = _dot(tb, (v * beta).astype(BF16))
        w = _dot(tb, (kb * jnp.exp(gc)).astype(BF16)).astype(BF16)
        aqk = jnp.where(lower, _dot_nt(q.astype(BF16), k.astype(BF16)) * decay, 0.0).astype(BF16)
        q_dec = (q * jnp.exp(gc)).astype(BF16)
        k_dec = (k * jnp.exp(rest)).astype(BF16)
        return u, w, aqk, q_dec, k_dec, jnp.exp(gc)

    terms = [chunk_terms(hh) for hh in range(heads)]
    states = [s_ref[hh] for hh in range(heads)]
    for ci in range(chunks):
        rows = slice(ci * c, (ci + 1) * c)
        for hh in range(heads):
            cols = slice(hh * HEAD_DIM, (hh + 1) * HEAD_DIM)
            u, w, aqk, q_dec, k_dec, chunk_decay = terms[hh]
            s = states[hh]
            sb = s.astype(BF16)
            v_new = u[rows, :] - _dot(w[rows, :], sb)
            o = _dot(q_dec[rows, :], sb) + _dot(aqk[rows, rows], v_new.astype(BF16))
            states[hh] = (s * chunk_decay[(ci + 1) * c - 1:(ci + 1) * c, :]
                          + _dot_tn(k_dec[rows, :], v_new.astype(BF16)))
            o_ref[rows, cols] = _rms(o, nw_ref[...]) * _silu(z_ref[rows, cols])
    for hh in range(heads):
        s_ref[hh] = states[hh]

    @pl.when(pl.program_id(2) == pl.num_programs(2) - 1)
    def _():
        sfin_ref[...] = s_ref[...]


def _gdn_chunked(q, k, v, gb, pc, norm_w, nseq, t, tm=256, heads=2):
    nt = t // tm
    width = heads * HEAD_DIM
    zb = C_GDN_Z // width
    head_spec = pl.BlockSpec((tm, width), lambda b, h, i: (b * nt + i, h))
    return pl.pallas_call(
        functools.partial(_gdn_chunk_kernel, chunks=tm // GDN_CHUNK, heads=heads),
        grid=(nseq, N_HEADS // heads, nt),
        in_specs=[head_spec, head_spec, head_spec, head_spec,
                  pl.BlockSpec((tm, width), lambda b, h, i: (b * nt + i, zb + h)),
                  pl.BlockSpec((1, HEAD_DIM), lambda b, h, i: (0, 0))],
        out_specs=[head_spec,
                   pl.BlockSpec((None, heads, HEAD_DIM, HEAD_DIM), lambda b, h, i: (b, h, 0, 0))],
        out_shape=[jax.ShapeDtypeStruct((nseq * t, GDN_W), F32),
                   jax.ShapeDtypeStruct((nseq, N_HEADS, HEAD_DIM, HEAD_DIM), F32)],
        scratch_shapes=[pltpu.VMEM((heads, HEAD_DIM, HEAD_DIM), F32)],
        compiler_params=_cparams("parallel", "parallel", "arbitrary"),
        name="gdn_chunked",
    )(q, k, v, gb, pc, norm_w.reshape(1, HEAD_DIM))


def _gdn_prep_sample_kernel(*refs):
    u_refs = refs[0:3]
    p_refs = refs[3:12]
    cw_ref, small_ref, alog_ref, dt_ref, bfox_ref = refs[12:17]
    q_ref, k_ref, v_ref, gb_ref, lf_ref = refs[17:22]

    def conv(idx):
        cw = cw_ref[:, idx * GDN_W:(idx + 1) * GDN_W]
        out = cw[GDN_CONV - 1:GDN_CONV, :] * u_refs[idx][...]
        for r in range(GDN_CONV - 1):
            out = out + cw[r:r + 1, :] * p_refs[3 * r + idx][...]
        return _silu(out)

    q_ref[...] = _l2norm_heads(conv(0), HEAD_DIM ** -0.5)
    k_ref[...] = _l2norm_heads(conv(1), 1.0)
    v_ref[...] = conv(2)
    gb_ref[...] = _gdn_gates(small_ref[...], alog_ref[...], dt_ref[...])
    lf_ref[...] = _log_sigmoid(small_ref[...] + bfox_ref[...])


def _gdn_prep_sample(pc, prev, conv_w, alog_row, dt_row, bfox_row):
    m = pc.shape[0]
    ub = C_GDN_QKV // GDN_W
    prev2 = prev.reshape(m, (GDN_CONV - 1) * 3 * GDN_W)
    row_spec = pl.BlockSpec((m, GDN_W), lambda i: (0, 0))
    par_spec = pl.BlockSpec((1, LANES), lambda i: (0, 0))
    in_specs = [pl.BlockSpec((m, GDN_W), functools.partial(lambda i, c: (0, c), c=ub + n)) for n in range(3)]
    in_specs += [pl.BlockSpec((m, GDN_W), functools.partial(lambda i, c: (0, c), c=n)) for n in range(9)]
    in_specs += [pl.BlockSpec((GDN_CONV, 3 * GDN_W), lambda i: (0, 0)),
                 pl.BlockSpec((m, LANES), lambda i: (0, C_SMALL // LANES)), par_spec, par_spec, par_spec]
    return pl.pallas_call(
        _gdn_prep_sample_kernel,
        grid=(1,),
        in_specs=in_specs,
        out_specs=[row_spec] * 4 + [pl.BlockSpec((m, LANES), lambda i: (0, 0))],
        out_shape=[jax.ShapeDtypeStruct((m, GDN_W), F32)] * 4 + [jax.ShapeDtypeStruct((m, LANES), F32)],
        compiler_params=_cparams("arbitrary"),
        name="gdn_prep_sample",
    )(pc, pc, pc, *([prev2] * 9), conv_w, pc, alog_row, dt_row, bfox_row)


def _gdn_step_kernel(q_ref, k_ref, v_ref, gb_ref, z_ref, nw_ref, s_ref, o_ref, so_ref, *, tb):
    pad = jnp.zeros((HEAD_DIM - tb, HEAD_DIM), F32)
    for h in range(N_HEADS):
        cols = slice(h * HEAD_DIM, (h + 1) * HEAD_DIM)
        k_t = jnp.concatenate([k_ref[:, cols], pad], axis=0).T
        q_t = jnp.concatenate([q_ref[:, cols], pad], axis=0).T
        for b in range(tb):
            decay = jnp.exp(gb_ref[b:b + 1, h * HEAD_DIM:h * HEAD_DIM + 1])
            beta = gb_ref[b:b + 1, h * HEAD_DIM + 1:h * HEAD_DIM + 2]
            kc = k_t[:, b:b + 1]
            s = s_ref[b, h] * decay
            v_old = jnp.sum(s * kc, axis=0, keepdims=True)
            s = s + kc * ((v_ref[b:b + 1, cols] - v_old) * beta)
            so_ref[b, h] = s
            o = jnp.sum(s * q_t[:, b:b + 1], axis=0, keepdims=True)
            o_ref[b:b + 1, cols] = _rms(o, nw_ref[...]) * _silu(z_ref[b:b + 1, cols])


def _gdn_step(q, k, v, gb, pc, norm_w, state, tb=8):
    m = q.shape[0]
    row_spec = pl.BlockSpec((tb, GDN_W), lambda i: (i, 0))
    st_spec = pl.BlockSpec((tb, N_HEADS, HEAD_DIM, HEAD_DIM), lambda i: (i, 0, 0, 0))
    return pl.pallas_call(
        functools.partial(_gdn_step_kernel, tb=tb),
        grid=(m // tb,),
        in_specs=[row_spec, row_spec, row_spec, row_spec,
                  pl.BlockSpec((tb, GDN_W), lambda i: (i, C_GDN_Z // GDN_W)),
                  pl.BlockSpec((1, HEAD_DIM), lambda i: (0, 0)), st_spec],
        out_specs=[row_spec, st_spec],
        out_shape=[jax.ShapeDtypeStruct((m, GDN_W), F32), jax.ShapeDtypeStruct(state.shape, F32)],
        compiler_params=_cparams("parallel"),
        name="gdn_step",
    )(q, k, v, gb, pc, norm_w.reshape(1, HEAD_DIM), state)


def _dec_idx_kernel(pt_ref, q_ref, w_ref, knew_ref, *refs, page):
    kidx_refs, (sc_ref, self_ref) = refs[:-2], refs[-2:]
    q = q_ref[...]
    qb = q.astype(BF16)
    w = w_ref[...] * (IDX_DIM ** -0.5 * IDX_HEADS ** -0.5)
    for pg, kidx_ref in enumerate(kidx_refs):
        si = _dot_nt(qb, kidx_ref[...].astype(BF16))
        sc_ref[:, pg * page:(pg + 1) * page] = jnp.sum(jnp.maximum(si, 0.0) * w, axis=0, keepdims=True)
    si_new = jnp.sum(q * knew_ref[...], axis=-1, keepdims=True)
    own = jnp.sum(jnp.maximum(si_new, 0.0) * w, axis=0, keepdims=True)
    self_ref[...] = jnp.broadcast_to(own, self_ref.shape)


def _dec_idx_scores(layer, page_table, qidx3, w3, knew3, cache_kidx):
    nb, npages = page_table.shape
    page = cache_kidx.shape[2]
    page_specs = [pl.BlockSpec((None, None, page, IDX_DIM),
                               functools.partial(lambda b, pt, pg: (layer, pt[b, pg], 0, 0), pg=pg))
                  for pg in range(npages)]
    grid_spec = pltpu.PrefetchScalarGridSpec(
        num_scalar_prefetch=1,
        grid=(nb,),
        in_specs=[pl.BlockSpec((None, IDX_HEADS, IDX_DIM), lambda b, pt: (b, 0, 0)),
                  pl.BlockSpec((None, IDX_HEADS, 1), lambda b, pt: (b, 0, 0)),
                  pl.BlockSpec((None, 1, IDX_DIM), lambda b, pt: (b, 0, 0))] + page_specs,
        out_specs=[pl.BlockSpec((None, 1, npages * page), lambda b, pt: (b, 0, 0)),
                   pl.BlockSpec((None, 1, LANES), lambda b, pt: (b, 0, 0))])
    return pl.pallas_call(
        functools.partial(_dec_idx_kernel, page=page),
        grid_spec=grid_spec,
        out_shape=[jax.ShapeDtypeStruct((nb, 1, npages * page), F32), jax.ShapeDtypeStruct((nb, 1, LANES), F32)],
        compiler_params=_cparams("parallel"),
        name="dec_idx_scores",
    )(page_table, qidx3, w3, knew3, *([cache_kidx] * npages))


def _dec_thr_kernel(sc_ref, self_ref, sel_ref, own_sel_ref, *, topk):
    scores = sc_ref[...]
    keys = _float_key(scores)
    own = _float_key(self_ref[:, 0:1])

    def count_ge(cand):
        cnt = jnp.sum(jnp.where(keys >= cand, 1, 0), axis=-1, keepdims=True)
        return cnt + jnp.where(own >= cand, 1, 0)

    nb, n = keys.shape
    res = jnp.maximum(_kth_largest_key(count_ge, nb, topk), KEY_NEG_INF)
    need = topk - count_ge(res + 1)
    pos = lax.broadcasted_iota(jnp.int32, (nb, n), 1)
    tied = keys == res

    def tied_before(c):
        return jnp.sum(jnp.where(tied & (pos < c), 1, 0), axis=-1, keepdims=True)

    cut = _tie_cut(tied_before, nb, need, n.bit_length())
    keep = (keys > res) | (tied & (pos <= cut))
    own_keep = (own > res) | ((own == res) & (tied_before(n) < need))
    own_sel_ref[...] = jnp.broadcast_to(jnp.where(own_keep, 1.0, 0.0), own_sel_ref.shape)
    r = lax.broadcasted_iota(jnp.int32, (LANES, N_KV * LANES), 0)
    c = lax.broadcasted_iota(jnp.int32, (LANES, N_KV * LANES), 1)
    spread = jnp.where(r == c // N_KV, 1.0, 0.0).astype(BF16)
    for j in range(n // LANES):
        sel = jnp.where(keep[:, j * LANES:(j + 1) * LANES], 1.0, 0.0).astype(BF16)
        sel_ref[:, j * N_KV * LANES:(j + 1) * N_KV * LANES] = _dot(sel, spread)


def _dec_threshold(scores, own, topk):
    nb, n = scores.shape
    return pl.pallas_call(
        functools.partial(_dec_thr_kernel, topk=topk),
        grid=(1,),
        in_specs=[pl.BlockSpec((nb, n), lambda i: (0, 0)), pl.BlockSpec((nb, LANES), lambda i: (0, 0))],
        out_specs=[pl.BlockSpec((nb, N_KV * n), lambda i: (0, 0)), pl.BlockSpec((nb, LANES), lambda i: (0, 0))],
        out_shape=[jax.ShapeDtypeStruct((nb, N_KV * n), F32), jax.ShapeDtypeStruct((nb, LANES), F32)],
        compiler_params=_cparams("arbitrary"),
        name="dec_threshold",
    )(scores, own)


N_CACHES = 6


def _dec_attn_kernel(pt_ref, fq_ref, sq_ref, dq_ref, fkn_ref, fvn_ref, dkn_ref, dvn_ref, lfn_ref, sel_ref, own_ref,
                     bias0_ref, bias_ref, fk_hbm, fv_hbm, sk_hbm, sv_hbm, dk_hbm, dv_hbm, lf_hbm,
                     of_ref, os_ref, od_ref, kv_buf, lf_buf, sems, *, layer, npages, page):
    b = pl.program_id(0)
    nb = pl.num_programs(0)
    rows = N_KV * page
    n = npages * rows
    caches = (fk_hbm, fv_hbm, sk_hbm, sv_hbm, dk_hbm, dv_hbm)
    scale = HEAD_DIM ** -0.5

    def page_copies(seq, slot):
        out = []
        for pg in range(npages):
            pid = pt_ref[seq, pg]
            for c, hbm in enumerate(caches):
                out.append(pltpu.make_async_copy(hbm.at[layer, pid], kv_buf.at[slot, c, pl.ds(pg * rows, rows)],
                                                 sems.at[slot, c]))
            out.append(pltpu.make_async_copy(lf_hbm.at[layer, pid], lf_buf.at[slot, pg], sems.at[slot, N_CACHES]))
        return out

    slot = b % 2

    @pl.when(b == 0)
    def _():
        for cp in page_copies(0, 0):
            cp.start()

    @pl.when(b + 1 < nb)
    def _():
        for cp in page_copies(b + 1, 1 - slot):
            cp.start()

    for cp in page_copies(b, slot):
        cp.wait()

    row = lax.broadcasted_iota(jnp.int32, (SUBLANES, n), 0)
    col = lax.broadcasted_iota(jnp.int32, (SUBLANES, n), 1)
    mine = (col % N_KV) == jnp.where(row < 2, 0, 1)
    top = lax.broadcasted_iota(jnp.int32, (SUBLANES, HEAD_DIM), 0) < 2

    def per_head(x2):
        return jnp.where(top, x2[0:1, :], x2[1:2, :])

    def slab(c):
        return kv_buf[slot, c].astype(BF16)

    rr = lax.broadcasted_iota(jnp.int32, (page, rows), 0)
    cc = lax.broadcasted_iota(jnp.int32, (page, rows), 1)
    pos_after = jnp.where(rr > cc // N_KV, 1.0, 0.0).astype(BF16)
    carry = lfn_ref[:, 0:1]
    decay = [None] * npages
    for pg in reversed(range(npages)):
        lf = lf_buf[slot, pg]
        hi, mid, lo = _split3(lf)
        decay[pg] = _dot(hi, pos_after) + _dot(mid, pos_after) + _dot(lo, pos_after) + carry
        carry = carry + jnp.sum(lf, axis=-1, keepdims=True)
    s = _dot_nt(fq_ref[...].astype(BF16), slab(0)) * scale + jnp.concatenate(decay, axis=-1)
    s = jnp.where(mine, s, MASKED)
    s_own = jnp.sum(fq_ref[...] * per_head(fkn_ref[...]), axis=-1, keepdims=True) * scale
    m = jnp.maximum(jnp.max(s, axis=-1, keepdims=True), s_own)
    w = jnp.where(mine, jnp.exp(s - m), 0.0)
    w_own = jnp.exp(s_own - m)
    acc = _dot(w.astype(BF16), slab(1)) + w_own * per_head(fvn_ref[...])
    of_ref[...] = acc / (jnp.sum(w, axis=-1, keepdims=True) + w_own)

    z = _dot_nt(sq_ref[...].astype(BF16), slab(2)) * scale
    sp = jnp.log(1.0 + jnp.exp(-jnp.abs(z)))
    log_take = jnp.minimum(z, 0.0) - sp
    log_stay = jnp.where(mine, jnp.minimum(-z, 0.0) - sp, 0.0)
    r2 = lax.broadcasted_iota(jnp.int32, (rows, rows), 0)
    c2 = lax.broadcasted_iota(jnp.int32, (rows, rows), 1)
    row_after = jnp.where(r2 // N_KV > c2 // N_KV, 1.0, 0.0).astype(BF16)
    carry = jnp.zeros((SUBLANES, 1), F32)
    later = [None] * npages
    for pg in reversed(range(npages)):
        ls = log_stay[:, pg * rows:(pg + 1) * rows]
        hi, lo = _split2(ls)
        later[pg] = _dot(hi, row_after) + _dot(lo, row_after) + carry
        carry = carry + jnp.sum(ls, axis=-1, keepdims=True)
    a = jnp.where(mine, jnp.exp(log_take + jnp.concatenate(later, axis=-1)), 0.0)
    os_ref[...] = _dot(a.astype(BF16), slab(3))

    keep = mine & (sel_ref[...] > 0.5)
    s = jnp.where(keep, _dot_nt(dq_ref[...].astype(BF16), slab(4)) * scale + bias_ref[...], MASKED)
    own_on = own_ref[:, 0:1] > 0.5
    d_own = jnp.sum(dq_ref[...] * per_head(dkn_ref[...]), axis=-1, keepdims=True) * scale + bias0_ref[:, 0:1]
    d_own = jnp.where(own_on, d_own, MASKED)
    m = jnp.maximum(jnp.max(s, axis=-1, keepdims=True), d_own)
    w = jnp.where(keep, jnp.exp(s - m), 0.0)
    w_own = jnp.where(own_on, jnp.exp(d_own - m), 0.0)
    acc = _dot(w.astype(BF16), slab(5)) + w_own * per_head(dvn_ref[...])
    od_ref[...] = acc / (jnp.sum(w, axis=-1, keepdims=True) + w_own)


def _dec_attention(layer, page_table, q8, new_kv, lfn8, sel3, own3, bias0, bias_dec, caches, lf_cache8):
    nb, npages = page_table.shape
    page = caches[0].shape[2] // N_KV
    n = npages * page * N_KV
    per_b8 = pl.BlockSpec((None, SUBLANES, HEAD_DIM), lambda b, pt: (b, 0, 0))
    per_b2 = pl.BlockSpec((None, N_KV, HEAD_DIM), lambda b, pt: (b, 0, 0))
    hbm = pl.BlockSpec(memory_space=pl.ANY)
    in_specs = [per_b8, per_b8, per_b8, per_b2, per_b2, per_b2, per_b2, per_b8,
                pl.BlockSpec((None, 1, n), lambda b, pt: (b, 0, 0)),
                pl.BlockSpec((None, 1, LANES), lambda b, pt: (b, 0, 0)),
                pl.BlockSpec((SUBLANES, LANES), lambda b, pt: (0, 0)),
                pl.BlockSpec((SUBLANES, n), lambda b, pt: (0, 0))] + [hbm] * (N_CACHES + 1)
    grid_spec = pltpu.PrefetchScalarGridSpec(
        num_scalar_prefetch=1, grid=(nb,), in_specs=in_specs,
        out_specs=[per_b8, per_b8, per_b8],
        scratch_shapes=[pltpu.VMEM((2, N_CACHES, n, HEAD_DIM), F32),
                        pltpu.VMEM((2, npages, SUBLANES, page), F32),
                        pltpu.SemaphoreType.DMA((2, N_CACHES + 1))])
    return pl.pallas_call(
        functools.partial(_dec_attn_kernel, layer=layer, npages=npages, page=page),
        grid_spec=grid_spec,
        out_shape=[jax.ShapeDtypeStruct((nb, SUBLANES, HEAD_DIM), F32)] * 3,
        compiler_params=_cparams("arbitrary"),
        name="dec_attention",
    )(page_table, *q8, *new_kv, lfn8, sel3, own3, bias0, bias_dec, *caches, lf_cache8)


def _lane_row(values, offset):
    return jnp.zeros((1, LANES), F32).at[0, offset:offset + values.shape[0]].set(values.astype(F32))


def _layer_params(l, w_mod, b_mod, g_mix_pre, g_mix_post, g_ffn_pre, g_ffn_post, w_in, b_fox_f, gdn_a_log,
                  gdn_dt_bias, gdn_conv_w, gdn_norm_w, w_out, w_up, ffn_conv_w, ffn_conv_b, w_down):
    segments, n_real = _in_proj_permutation(w_in.shape[1])
    parts = [w_in[l][:, s:s + n].astype(BF16) for s, n in segments]
    w_in_p = jnp.concatenate(parts + [jnp.zeros((w_in.shape[1], D_IN_PAD - n_real), BF16)], axis=1)
    return dict(
        w_mod=w_mod[l], b_mod=b_mod[l], g_mix_pre=g_mix_pre[l], g_mix_post=g_mix_post[l],
        g_ffn_pre=g_ffn_pre[l], g_ffn_post=g_ffn_post[l], w_in=w_in_p,
        b_fox_row=_lane_row(b_fox_f[l], S_FOX_F), alog_row=_lane_row(gdn_a_log[l], S_GDN_A),
        dt_row=_lane_row(gdn_dt_bias[l], S_GDN_A), gdn_conv_w=gdn_conv_w[l], gdn_norm_w=gdn_norm_w[l],
        w_out=w_out[l].astype(BF16), w_up=w_up[l].astype(BF16), ffn_conv_w=ffn_conv_w[l],
        ffn_conv_b=ffn_conv_b[l], w_down=w_down[l].astype(BF16))


def _kv_state(pc, col, lead):
    return pc[:, col:col + N_KV * HEAD_DIM].reshape(*lead, N_KV, HEAD_DIM)


def _prompt_layer(x, mod, p, rel_tiles, nseq, t):
    pc = _in_proj(x, p['g_mix_pre'], mod, p['w_in'], t, tm=1024, tn=512)
    lf, cum, cum_rep = _fox_gates(pc, p['b_fox_row'], nseq, t)
    cum_rows = jnp.transpose(cum[:, :N_HEADS].reshape(nseq, t, N_KV, 2), (0, 2, 3, 1)).reshape(nseq * N_KV, 2, t)
    o_fox = _flash(pc, nseq, t, 'fox', (cum_rep, cum_rows))
    o_sb = _sb_attention(pc, nseq, t)
    gq, gk, gv, gb, gdn_conv = _gdn_prep_prompt(pc, p['gdn_conv_w'], p['alog_row'], p['dt_row'], nseq, t)
    o_gdn, gdn_s = _gdn_chunked(gq, gk, gv, gb, pc, p['gdn_norm_w'], nseq, t)
    scores, thr = _dsa_scores(pc, nseq, t, min(TOPK_MAX, t // 4))
    o_dsa = _flash(pc, nseq, t, 'dsa', (rel_tiles, scores, thr))
    x = _out_proj((o_fox, o_sb, o_gdn, o_dsa), p['w_out'], x, p['g_mix_post'], mod, t, tm=256)
    x, ffn_conv = _ffn_prompt(x, p['g_ffn_pre'], mod, p['w_up'], p['ffn_conv_w'], p['ffn_conv_b'], p['w_down'],
                              p['g_ffn_post'], t, tm=512, tf=512)
    lead = (nseq, t)
    states = (_kv_state(pc, C_FOX_K, lead), _kv_state(pc, C_FOX_V, lead), lf[:, :N_HEADS].reshape(nseq, t, N_HEADS),
              _kv_state(pc, C_SB_K, lead), _kv_state(pc, C_SB_V, lead),
              _kv_state(pc, C_DSA_K, lead), _kv_state(pc, C_DSA_V, lead),
              pc[:, C_DSA_KIDX:C_DSA_KIDX + IDX_DIM].reshape(nseq, t, IDX_DIM), gdn_s, gdn_conv, ffn_conv)
    return x, states


def _heads8(x):
    nb = x.shape[0]
    return jnp.pad(x.reshape(nb, N_HEADS, HEAD_DIM), ((0, 0), (0, SUBLANES - N_HEADS), (0, 0)))


def _sample_layer(layer, x, mod, p, caches, lf_cache8, kidx_cache, page_table, bias_dec, bias0,
                  gdn_s, gdn_conv, ffn_conv):
    nb = x.shape[0]
    pc = _in_proj(x, p['g_mix_pre'], mod, p['w_in'], 1, tm=nb, tn=512)

    def cols(c, n):
        return pc[:, c:c + n]

    gq, gk, gv, gb, lf = _gdn_prep_sample(pc, gdn_conv, p['gdn_conv_w'], p['alog_row'], p['dt_row'], p['b_fox_row'])
    o_gdn, gdn_s_new = _gdn_step(gq, gk, gv, gb, pc, p['gdn_norm_w'], gdn_s)
    gdn_conv_new = jnp.concatenate([gdn_conv[:, 1:], cols(C_GDN_QKV, 3 * GDN_W)[:, None]], axis=1)

    qidx3 = cols(C_DSA_QIDX, IDX_HEADS * IDX_DIM).reshape(nb, IDX_HEADS, IDX_DIM)
    w3 = cols(C_SMALL + S_DSA_W, IDX_HEADS).reshape(nb, IDX_HEADS, 1)
    knew3 = cols(C_DSA_KIDX, IDX_DIM).reshape(nb, 1, IDX_DIM)
    scores3, own3 = _dec_idx_scores(layer, page_table, qidx3, w3, knew3, kidx_cache)
    past = scores3.shape[-1]
    sel, own_sel = _dec_threshold(scores3.reshape(nb, past), own3.reshape(nb, LANES),
                                  min(TOPK_MAX, (past + 1) // 4))

    lf4 = lf[:, :N_HEADS]
    lfn8 = jnp.broadcast_to(jnp.pad(lf4, ((0, 0), (0, SUBLANES - N_HEADS)))[:, :, None], (nb, SUBLANES, LANES))
    q8 = (_heads8(cols(C_FOX_Q, 512)), _heads8(cols(C_SB_Q, 512)), _heads8(cols(C_DSA_Q, 512)))
    new_kv = tuple(cols(c, 256).reshape(nb, N_KV, HEAD_DIM) for c in (C_FOX_K, C_FOX_V, C_DSA_K, C_DSA_V))
    o8 = _dec_attention(layer, page_table, q8, new_kv, lfn8, sel.reshape(nb, 1, N_KV * past),
                        own_sel.reshape(nb, 1, LANES), bias0, bias_dec, caches, lf_cache8)
    o_fox, o_sb, o_dsa = (o[:, :N_HEADS].reshape(nb, N_HEADS * HEAD_DIM) for o in o8)

    x = _out_proj((o_fox, o_sb, o_gdn, o_dsa), p['w_out'], x, p['g_mix_post'], mod, 1, tm=nb)
    x, up = _ffn_sample(x, p['g_ffn_pre'], mod, p['w_up'], p['ffn_conv_w'], p['ffn_conv_b'], p['w_down'],
                        p['g_ffn_post'], ffn_conv, tf=512)
    ffn_conv_new = jnp.concatenate([ffn_conv[:, 1:], up[:, None]], axis=1)
    lead = (nb, 1)
    states = (_kv_state(pc, C_FOX_K, lead), _kv_state(pc, C_FOX_V, lead), lf4.reshape(nb, 1, N_HEADS),
              _kv_state(pc, C_SB_K, lead), _kv_state(pc, C_SB_V, lead),
              _kv_state(pc, C_DSA_K, lead), _kv_state(pc, C_DSA_V, lead),
              cols(C_DSA_KIDX, IDX_DIM).reshape(nb, 1, IDX_DIM), gdn_s_new, gdn_conv_new, ffn_conv_new)
    return x, states


def kernel(x_prompt, x_sample, cache_fox_k, cache_fox_v, cache_fox_logf, cache_sb_k, cache_sb_v, cache_dsa_k,
           cache_dsa_v, cache_dsa_kidx, state_gdn_s, state_gdn_conv, state_ffn_conv, page_table, c_prompt,
           c_sample, w_mod, b_mod, g_mix_pre, g_mix_post, g_ffn_pre, g_ffn_post, w_in, b_fox_f, gdn_a_log,
           gdn_dt_bias, gdn_conv_w, gdn_norm_w, rel_bias, w_out, w_up, ffn_conv_w, ffn_conv_b, w_down):
    nseq, t, d = x_prompt.shape
    nsamp = x_sample.shape[0]
    depth = w_in.shape[0]
    xp = x_prompt.reshape(nseq * t, d)
    xs = x_sample.reshape(nsamp, d)
    pad = (-(nseq + nsamp)) % SUBLANES
    c_all = jnp.concatenate([c_prompt, c_sample, jnp.zeros((pad, d), F32)], axis=0)
    rel_tiles = _rel_bias_tiles(rel_bias, ATTN_TILE)

    pool_shape = cache_fox_k.shape[:3]
    caches = tuple(c.reshape(pool_shape[0], pool_shape[1], pool_shape[2] * N_KV, HEAD_DIM)
                   for c in (cache_fox_k, cache_fox_v, cache_sb_k, cache_sb_v, cache_dsa_k, cache_dsa_v))
    lf_cache8 = jnp.pad(jnp.swapaxes(cache_fox_logf, 2, 3), ((0, 0), (0, 0), (0, SUBLANES - N_HEADS), (0, 0)))
    past = page_table.shape[1] * pool_shape[2]
    rel_dec = past - jnp.arange(N_KV * past, dtype=jnp.int32) // N_KV
    bias_dec = jnp.pad(_bucket_bias(rel_bias, _t5_bucket(rel_dec)), ((0, SUBLANES - N_HEADS), (0, 0)))
    bias0 = jnp.broadcast_to(jnp.pad(rel_bias[0].astype(F32), (0, SUBLANES - N_HEADS))[:, None], (SUBLANES, LANES))

    st_prompt, st_sample = [], []
    for l in range(depth):
        p = _layer_params(l, w_mod, b_mod, g_mix_pre, g_mix_post, g_ffn_pre, g_ffn_post, w_in, b_fox_f, gdn_a_log,
                          gdn_dt_bias, gdn_conv_w, gdn_norm_w, w_out, w_up, ffn_conv_w, ffn_conv_b, w_down)
        mod = _modulation(c_all, p['w_mod'], p['b_mod'])
        xp, sp = _prompt_layer(xp, mod[:nseq], p, rel_tiles, nseq, t)
        xs, ss = _sample_layer(l, xs, mod[nseq:nseq + nsamp], p, caches, lf_cache8, cache_dsa_kidx, page_table,
                               bias_dec, bias0, state_gdn_s[l], state_gdn_conv[l], state_ffn_conv[l])
        st_prompt.append(sp)
        st_sample.append(ss)
    sp = [jnp.stack(z) for z in zip(*st_prompt)]
    ss = [jnp.stack(z) for z in zip(*st_sample)]
    out = [xp.reshape(nseq, t, d), xs.reshape(nsamp, 1, d)]
    for a, b in zip(sp, ss):
        out += [a, b]
    return tuple(out)
```

```python
import functools
import math

import numpy as np
import jax
import jax.numpy as jnp
from jax import lax
from jax.experimental import pallas as pl
from jax.experimental.pallas import tpu as pltpu

F32 = jnp.float32
BF16 = jnp.bfloat16

HEAD_DIM = 128
N_HEADS = 4
N_KV = 2
IDX_HEADS = 16
IDX_DIM = 128
GDN_CONV = 4
GDN_CHUNK = 64
FFN_CONV = 3
TOPK_MAX = 256
N_BUCKETS = 32
MAX_DISTANCE = 128
EPS = 1e-6
NEG_INF = float("-inf")

LANES = 128
SUBLANES = 8
VMEM_LIMIT = 56 * 1024 * 1024
ATTN_TILE = 512

C_FOX_Q, C_FOX_K, C_FOX_V = 0, 512, 768
C_SB_Q, C_SB_K, C_SB_V = 1024, 1536, 1792
C_GDN_QKV, C_GDN_Z = 2048, 3584
C_DSA_QIDX = 4096
C_DSA_Q, C_DSA_K, C_DSA_V = 6144, 6656, 6912
C_DSA_KIDX, C_SMALL = 7168, 7296
D_IN_PAD = 7680
S_FOX_F, S_GDN_A, S_GDN_B, S_DSA_W = 0, 4, 8, 12


def _in_proj_permutation(d_model):
    gdn_qkv = N_HEADS * 3 * HEAD_DIM
    layout = (
        ('fox_q', 512), ('fox_k', 256), ('fox_v', 256), ('fox_f', 4),
        ('sb_q', 512), ('sb_k', 256), ('sb_v', 256),
        ('gdn_qkv', gdn_qkv), ('gdn_a', 4), ('gdn_b', 4), ('gdn_z', 512),
        ('dsa_q', 512), ('dsa_k', 256), ('dsa_v', 256),
        ('dsa_qidx', IDX_HEADS * IDX_DIM), ('dsa_kidx', IDX_DIM), ('dsa_w', IDX_HEADS),
    )
    off = {}
    o = 0
    for name, size in layout:
        off[name] = (o, size)
        o += size
    order = ('fox_q', 'fox_k', 'fox_v', 'sb_q', 'sb_k', 'sb_v', 'gdn_qkv', 'gdn_z', 'dsa_qidx', 'dsa_q', 'dsa_k',
             'dsa_v', 'dsa_kidx', 'fox_f', 'gdn_a', 'gdn_b', 'dsa_w')
    return [off[n] for n in order], o


def _cparams(*sem):
    return pltpu.CompilerParams(dimension_semantics=sem, vmem_limit_bytes=VMEM_LIMIT)


def _rms(x, g):
    return x * lax.rsqrt(jnp.mean(x * x, axis=-1, keepdims=True) + EPS) * g


def _silu(x):
    return x * (1.0 / (1.0 + jnp.exp(-x)))


def _log_sigmoid(x):
    return jnp.minimum(x, 0.0) - jnp.log(1.0 + jnp.exp(-jnp.abs(x)))


def _split3(x):
    hi = x.astype(BF16)
    r1 = x - hi.astype(F32)
    mid = r1.astype(BF16)
    lo = (r1 - mid.astype(F32)).astype(BF16)
    return hi, mid, lo


def _dot(a, b):
    return jnp.dot(a, b, preferred_element_type=F32)


def _dot_nt(a, b):
    return lax.dot_general(a, b, (((1,), (1,)), ((), ())), preferred_element_type=F32)


def _dot_tn(a, b):
    return lax.dot_general(a, b, (((0,), (0,)), ((), ())), preferred_element_type=F32)


def _mod_kernel(c_ref, w_ref, b_ref, o_ref):
    a = _silu(c_ref[...]).astype(BF16)
    o_ref[...] = _dot(a, w_ref[...].astype(BF16)) + b_ref[...]


def _modulation(c, w_mod, b_mod):
    m, d = c.shape
    n = w_mod.shape[1]
    tn = 1024
    return pl.pallas_call(
        _mod_kernel,
        grid=(n // tn,),
        in_specs=[pl.BlockSpec((m, d), lambda j: (0, 0)),
                  pl.BlockSpec((d, tn), lambda j: (0, j)),
                  pl.BlockSpec((1, tn), lambda j: (0, j))],
        out_specs=pl.BlockSpec((m, tn), lambda j: (0, j)),
        out_shape=jax.ShapeDtypeStruct((m, n), F32),
        compiler_params=_cparams("parallel"),
        name="modulation",
    )(c, w_mod, b_mod.reshape(1, n))


def _mod_spec(mod, chunk, tm, rows_per_seq, d):
    if rows_per_seq == 1:
        return mod, pl.BlockSpec((tm, d), lambda i, *_: (i, chunk))
    tiles_per_seq = rows_per_seq // tm
    return (mod.reshape(mod.shape[0], 1, mod.shape[1]),
            pl.BlockSpec((None, 1, d), lambda i, *_: (i // tiles_per_seq, 0, chunk)))


def _in_proj_kernel(x_ref, g_ref, sc_ref, sh_ref, w_ref, o_ref, h_ref):
    @pl.when(pl.program_id(1) == 0)
    def _():
        h = _rms(x_ref[...], g_ref[...]) * (1.0 + sc_ref[...]) + sh_ref[...]
        h_ref[...] = h.astype(BF16)

    o_ref[...] = _dot(h_ref[...], w_ref[...])


def _in_proj(x, g, mod, w, rows_per_seq, tm, tn):
    m, d = x.shape
    n = w.shape[1]
    sc, sc_spec = _mod_spec(mod, 1, tm, rows_per_seq, d)
    sh, sh_spec = _mod_spec(mod, 0, tm, rows_per_seq, d)
    return pl.pallas_call(
        _in_proj_kernel,
        grid=(m // tm, n // tn),
        in_specs=[pl.BlockSpec((tm, d), lambda i, j: (i, 0)),
                  pl.BlockSpec((1, d), lambda i, j: (0, 0)),
                  sc_spec, sh_spec,
                  pl.BlockSpec((d, tn), lambda i, j: (0, j))],
        out_specs=pl.BlockSpec((tm, tn), lambda i, j: (i, j)),
        out_shape=jax.ShapeDtypeStruct((m, n), F32),
        scratch_shapes=[pltpu.VMEM((tm, d), BF16)],
        compiler_params=_cparams("parallel", "arbitrary"),
        name="in_proj",
    )(x, g.reshape(1, d), sc, sh, w)


def _out_proj_kernel(a0_ref, a1_ref, a2_ref, a3_ref, w_ref, x_ref, g_ref, gate_ref, o_ref):
    m = None
    for n, a_ref in enumerate((a0_ref, a1_ref, a2_ref, a3_ref)):
        k = a_ref.shape[1]
        part = _dot(a_ref[...].astype(BF16), w_ref[n * k:(n + 1) * k, :])
        m = part if m is None else m + part
    o_ref[...] = x_ref[...] + gate_ref[...] * _rms(m, g_ref[...])


def _out_proj(mixed, w, x, g, mod, rows_per_seq, tm):
    m, k4 = mixed[0].shape
    k, d = w.shape
    gate, gate_spec = _mod_spec(mod, 2, tm, rows_per_seq, d)
    a_spec = pl.BlockSpec((tm, k4), lambda i: (i, 0))
    return pl.pallas_call(
        _out_proj_kernel,
        grid=(m // tm,),
        in_specs=[a_spec, a_spec, a_spec, a_spec,
                  pl.BlockSpec((k, d), lambda i: (0, 0)),
                  pl.BlockSpec((tm, d), lambda i: (i, 0)),
                  pl.BlockSpec((1, d), lambda i: (0, 0)),
                  gate_spec],
        out_specs=pl.BlockSpec((tm, d), lambda i: (i, 0)),
        out_shape=jax.ShapeDtypeStruct((m, d), F32),
        compiler_params=_cparams("parallel"),
        name="out_proj",
    )(*mixed, w, x, g.reshape(1, d), gate)


def _shift_rows(u, prev, k):
    rolled = pltpu.roll(u, k, axis=0)
    row = lax.broadcasted_iota(jnp.int32, u.shape, 0)
    out = rolled
    for r in range(k):
        out = jnp.where(row == r, prev[SUBLANES - k + r:SUBLANES - k + r + 1, :], out)
    return out


def _ffn_prompt_kernel(x_ref, g_ref, sc_ref, sh_ref, wg_ref, wv_ref, cwg_ref, cwv_ref, cbg_ref, cbv_ref,
                       wd_ref, gpost_ref, gate_ref, o_ref, sg_ref, sv_ref,
                       h_ref, acc_ref, cg_ref, cv_ref, *, tiles_per_seq):
    i = pl.program_id(0)
    j = pl.program_id(1)

    @pl.when(j == 0)
    def _():
        h = _rms(x_ref[...], g_ref[...]) * (1.0 + sc_ref[...]) + sh_ref[...]
        h_ref[...] = h.astype(BF16)
        acc_ref[...] = jnp.zeros_like(acc_ref)

    fresh = i % tiles_per_seq == 0

    def conv(u, carry_ref, cw_ref, cb_ref):
        prev = jnp.where(fresh, 0.0, carry_ref[j])
        out = cw_ref[0:1, :] * _shift_rows(u, prev, 2) + cw_ref[1:2, :] * _shift_rows(u, prev, 1)
        out = out + cw_ref[2:3, :] * u + cb_ref[...]
        carry_ref[j] = u[u.shape[0] - SUBLANES:, :]
        return out

    ug = _dot(h_ref[...], wg_ref[...])
    uv = _dot(h_ref[...], wv_ref[...])
    sg_ref[...] = ug[ug.shape[0] - SUBLANES:, :]
    sv_ref[...] = uv[uv.shape[0] - SUBLANES:, :]
    gate = conv(ug, cg_ref, cwg_ref, cbg_ref)
    val = conv(uv, cv_ref, cwv_ref, cbv_ref)
    hid = (_silu(gate) * val).astype(BF16)
    acc_ref[...] += _dot(hid, wd_ref[...])

    @pl.when(j == pl.num_programs(1) - 1)
    def _():
        o_ref[...] = x_ref[...] + gate_ref[...] * _rms(acc_ref[...], gpost_ref[...])


def _ffn_prompt(x, g_pre, mod, w_up, conv_w, conv_b, w_down, g_post, rows_per_seq, tm, tf):
    m, d = x.shape
    f = w_down.shape[0]
    nf = f // tf
    nseq = m // rows_per_seq
    tiles_per_seq = rows_per_seq // tm
    sc, sc_spec = _mod_spec(mod, 4, tm, rows_per_seq, d)
    sh, sh_spec = _mod_spec(mod, 3, tm, rows_per_seq, d)
    gate, gate_spec = _mod_spec(mod, 5, tm, rows_per_seq, d)
    state_spec = pl.BlockSpec((None, SUBLANES, tf), lambda i, j: (i, 0, j))
    out, sg, sv = pl.pallas_call(
        functools.partial(_ffn_prompt_kernel, tiles_per_seq=tiles_per_seq),
        grid=(m // tm, nf),
        in_specs=[pl.BlockSpec((tm, d), lambda i, j: (i, 0)),
                  pl.BlockSpec((1, d), lambda i, j: (0, 0)),
                  sc_spec, sh_spec,
                  pl.BlockSpec((d, tf), lambda i, j: (0, j)),
                  pl.BlockSpec((d, tf), lambda i, j: (0, j + nf)),
                  pl.BlockSpec((FFN_CONV, tf), lambda i, j: (0, j)),
                  pl.BlockSpec((FFN_CONV, tf), lambda i, j: (0, j + nf)),
                  pl.BlockSpec((1, tf), lambda i, j: (0, j)),
                  pl.BlockSpec((1, tf), lambda i, j: (0, j + nf)),
                  pl.BlockSpec((tf, d), lambda i, j: (j, 0)),
                  pl.BlockSpec((1, d), lambda i, j: (0, 0)),
                  gate_spec],
        out_specs=[pl.BlockSpec((tm, d), lambda i, j: (i, 0)), state_spec, state_spec],
        out_shape=[jax.ShapeDtypeStruct((m, d), F32),
                   jax.ShapeDtypeStruct((m // tm, SUBLANES, f), F32),
                   jax.ShapeDtypeStruct((m // tm, SUBLANES, f), F32)],
        scratch_shapes=[pltpu.VMEM((tm, d), BF16), pltpu.VMEM((tm, d), F32),
                        pltpu.VMEM((nf, SUBLANES, tf), F32), pltpu.VMEM((nf, SUBLANES, tf), F32)],
        compiler_params=_cparams("arbitrary", "arbitrary"),
        name="ffn_prompt",
    )(x, g_pre.reshape(1, d), sc, sh, w_up, w_up, conv_w, conv_w, conv_b.reshape(1, 2 * f),
      conv_b.reshape(1, 2 * f), w_down, g_post.reshape(1, d), gate)
    state = jnp.concatenate([sg, sv], axis=-1)[tiles_per_seq - 1::tiles_per_seq, SUBLANES - (FFN_CONV - 1):, :]
    return out, state


def _ffn_sample_kernel(x_ref, g_ref, sc_ref, sh_ref, wg_ref, wv_ref, cwg_ref, cwv_ref, cbg_ref, cbv_ref,
                       p0g_ref, p0v_ref, p1g_ref, p1v_ref, wd_ref, gpost_ref, gate_ref,
                       o_ref, ug_ref, uv_ref, h_ref, acc_ref):
    j = pl.program_id(1)

    @pl.when(j == 0)
    def _():
        h = _rms(x_ref[...], g_ref[...]) * (1.0 + sc_ref[...]) + sh_ref[...]
        h_ref[...] = h.astype(BF16)
        acc_ref[...] = jnp.zeros_like(acc_ref)

    ug = _dot(h_ref[...], wg_ref[...])
    uv = _dot(h_ref[...], wv_ref[...])
    ug_ref[...] = ug
    uv_ref[...] = uv
    gate = cwg_ref[0:1, :] * p0g_ref[...] + cwg_ref[1:2, :] * p1g_ref[...] + cwg_ref[2:3, :] * ug + cbg_ref[...]
    val = cwv_ref[0:1, :] * p0v_ref[...] + cwv_ref[1:2, :] * p1v_ref[...] + cwv_ref[2:3, :] * uv + cbv_ref[...]
    hid = (_silu(gate) * val).astype(BF16)
    acc_ref[...] += _dot(hid, wd_ref[...])

    @pl.when(j == pl.num_programs(1) - 1)
    def _():
        o_ref[...] = x_ref[...] + gate_ref[...] * _rms(acc_ref[...], gpost_ref[...])


def _ffn_sample(x, g_pre, mod, w_up, conv_w, conv_b, w_down, g_post, prev, tf):
    m, d = x.shape
    f = w_down.shape[0]
    nf = f // tf
    tm = m
    sc, sc_spec = _mod_spec(mod, 4, tm, 1, d)
    sh, sh_spec = _mod_spec(mod, 3, tm, 1, d)
    gate, gate_spec = _mod_spec(mod, 5, tm, 1, d)
    prev2 = prev.reshape(m, (FFN_CONV - 1) * 2 * f)
    up_spec = pl.BlockSpec((tm, tf), lambda i, j: (i, j))
    out, up_g, up_v = pl.pallas_call(
        _ffn_sample_kernel,
        grid=(1, nf),
        in_specs=[pl.BlockSpec((tm, d), lambda i, j: (i, 0)),
                  pl.BlockSpec((1, d), lambda i, j: (0, 0)),
                  sc_spec, sh_spec,
                  pl.BlockSpec((d, tf), lambda i, j: (0, j)),
                  pl.BlockSpec((d, tf), lambda i, j: (0, j + nf)),
                  pl.BlockSpec((FFN_CONV, tf), lambda i, j: (0, j)),
                  pl.BlockSpec((FFN_CONV, tf), lambda i, j: (0, j + nf)),
                  pl.BlockSpec((1, tf), lambda i, j: (0, j)),
                  pl.BlockSpec((1, tf), lambda i, j: (0, j + nf)),
                  pl.BlockSpec((tm, tf), lambda i, j: (i, j)),
                  pl.BlockSpec((tm, tf), lambda i, j: (i, j + nf)),
                  pl.BlockSpec((tm, tf), lambda i, j: (i, j + 2 * nf)),
                  pl.BlockSpec((tm, tf), lambda i, j: (i, j + 3 * nf)),
                  pl.BlockSpec((tf, d), lambda i, j: (j, 0)),
                  pl.BlockSpec((1, d), lambda i, j: (0, 0)),
                  gate_spec],
        out_specs=[pl.BlockSpec((tm, d), lambda i, j: (i, 0)), up_spec, up_spec],
        out_shape=[jax.ShapeDtypeStruct((m, d), F32), jax.ShapeDtypeStruct((m, f), F32),
                   jax.ShapeDtypeStruct((m, f), F32)],
        scratch_shapes=[pltpu.VMEM((tm, d), BF16), pltpu.VMEM((tm, d), F32)],
        compiler_params=_cparams("arbitrary", "arbitrary"),
        name="ffn_sample",
    )(x, g_pre.reshape(1, d), sc, sh, w_up, w_up, conv_w, conv_w, conv_b.reshape(1, 2 * f),
      conv_b.reshape(1, 2 * f), prev2, prev2, prev2, prev2, w_down, g_post.reshape(1, d), gate)
    return out, jnp.concatenate([up_g, up_v], axis=-1)


def _tri(n, fn):
    r = lax.broadcasted_iota(jnp.int32, (n, n), 0)
    c = lax.broadcasted_iota(jnp.int32, (n, n), 1)
    return jnp.where(fn(r, c), 1.0, 0.0).astype(BF16)


def _dot_exact01(m01, x):
    hi, mid, lo = _split3(x)
    return _dot(m01, hi) + _dot(m01, mid) + _dot(m01, lo)


def _fox_gate_kernel(s_ref, b_ref, lf_ref, cum_ref, rep_ref, carry_ref):
    @pl.when(pl.program_id(1) == 0)
    def _():
        carry_ref[...] = jnp.zeros_like(carry_ref)

    lf = _log_sigmoid(s_ref[...] + b_ref[...])
    n = lf.shape[0]
    cum = _dot_exact01(_tri(n, lambda r, c: r >= c), lf) + carry_ref[0:1, :]
    lf_ref[...] = lf
    cum_ref[...] = cum
    for h in range(N_HEADS):
        rep_ref[:, h * LANES:(h + 1) * LANES] = jnp.broadcast_to(cum[:, S_FOX_F + h:S_FOX_F + h + 1], (n, LANES))
    carry_ref[...] = jnp.broadcast_to(cum[n - 1:n, :], carry_ref.shape)


def _fox_gates(pc, b_row, nseq, t, tc=256):
    nt = t // tc
    cb = C_SMALL // LANES
    spec = pl.BlockSpec((tc, LANES), lambda b, i: (b * nt + i, 0))
    return pl.pallas_call(
        _fox_gate_kernel,
        grid=(nseq, nt),
        in_specs=[pl.BlockSpec((tc, LANES), lambda b, i: (b * nt + i, cb)),
                  pl.BlockSpec((1, LANES), lambda b, i: (0, 0))],
        out_specs=[spec, spec, pl.BlockSpec((tc, N_HEADS * LANES), lambda b, i: (b * nt + i, 0))],
        out_shape=[jax.ShapeDtypeStruct((nseq * t, LANES), F32)] * 2
        + [jax.ShapeDtypeStruct((nseq * t, N_HEADS * LANES), F32)],
        scratch_shapes=[pltpu.VMEM((SUBLANES, LANES), F32)],
        compiler_params=_cparams("parallel", "arbitrary"),
        name="fox_gates",
    )(pc, b_row)


MASKED = -1e30


def _flash_kernel(*refs, mode, tq, tk):
    if mode == 'fox':
        q_ref, k_ref, v_ref, cq_ref, ck_ref, o_ref, m_ref, acc_ref = refs
    else:
        q_ref, k_ref, v_ref, bias_ref, sc_ref, thr_ref, o_ref, m_ref, acc_ref = refs
    qi = pl.program_id(2)
    ki = pl.program_id(3)

    @pl.when(ki == 0)
    def _():
        m_ref[...] = jnp.full_like(m_ref, MASKED)
        acc_ref[...] = jnp.zeros_like(acc_ref)

    def step(diagonal):
        t_pos = qi * tq + lax.broadcasted_iota(jnp.int32, (tq, LANES), 0)
        lane = lax.broadcasted_iota(jnp.int32, (tq, LANES), 1)
        masks = []
        for c in range(tk // LANES):
            mk = ((ki * tk + c * LANES + lane) <= t_pos) if diagonal else None
            if mode == 'dsa':
                sel = sc_ref[:, c * LANES:(c + 1) * LANES] >= thr_ref[...]
                mk = sel if mk is None else mk & sel
            masks.append(mk)
        kb = k_ref[...].astype(BF16)
        v_ext = jnp.concatenate([v_ref[...].astype(BF16), jnp.ones((tk, LANES), BF16)], axis=1)
        for g in range(2):
            q = q_ref[:, g * HEAD_DIM:(g + 1) * HEAD_DIM].astype(BF16)
            s = _dot_nt(q, kb) * HEAD_DIM ** -0.5
            blocks = []
            for c, mk in enumerate(masks):
                blk = s[:, c * LANES:(c + 1) * LANES]
                if mode == 'fox':
                    blk = blk + (cq_ref[:, g * LANES:(g + 1) * LANES] - ck_ref[g:g + 1, c * LANES:(c + 1) * LANES])
                else:
                    blk = blk + bias_ref[g, :, c * LANES:(c + 1) * LANES]
                blocks.append(blk if mk is None else jnp.where(mk, blk, MASKED))
            top = functools.reduce(jnp.maximum, blocks)
            m_old = m_ref[g]
            m_new = jnp.maximum(m_old, jnp.max(top, axis=-1, keepdims=True))
            alpha = jnp.exp(m_old - m_new)
            p = jnp.concatenate([jnp.exp(blk - m_new) if mk is None else jnp.where(mk, jnp.exp(blk - m_new), 0.0)
                                 for blk, mk in zip(blocks, masks)], axis=1)
            pv = _dot(p.astype(BF16), v_ext)
            acc_ref[g, :, 0:LANES] = alpha * acc_ref[g, :, 0:LANES] + pv[:, 0:LANES]
            acc_ref[g, :, LANES:2 * LANES] = alpha * acc_ref[g, :, LANES:2 * LANES] + pv[:, LANES:2 * LANES]
            m_ref[g] = m_new

    @pl.when(ki < qi)
    def _():
        step(False)

    @pl.when(ki == qi)
    def _():
        step(True)
        for g in range(2):
            o_ref[:, g * HEAD_DIM:(g + 1) * HEAD_DIM] = acc_ref[g, :, 0:LANES] / acc_ref[g, :, LANES:2 * LANES]


def _flash(pc, nseq, t, mode, extra, tq=ATTN_TILE):
    tk = tq
    nq = t // tq
    c_q, c_k, c_v = (C_FOX_Q, C_FOX_K, C_FOX_V) if mode == 'fox' else (C_DSA_Q, C_DSA_K, C_DSA_V)
    qb, kb, vb = c_q // 256, c_k // HEAD_DIM, c_v // HEAD_DIM
    in_specs = [pl.BlockSpec((tq, 256), lambda b, h, i, j: (b * nq + i, qb + h)),
                pl.BlockSpec((tk, HEAD_DIM), lambda b, h, i, j: (b * nq + jnp.minimum(i, j), kb + h)),
                pl.BlockSpec((tk, HEAD_DIM), lambda b, h, i, j: (b * nq + jnp.minimum(i, j), vb + h))]
    if mode == 'fox':
        in_specs += [pl.BlockSpec((tq, 2 * LANES), lambda b, h, i, j: (b * nq + i, h)),
                     pl.BlockSpec((None, 2, tk), lambda b, h, i, j: (b * N_KV + h, 0, jnp.minimum(i, j)))]
    else:
        in_specs += [pl.BlockSpec((None, 2, None, tq, tk),
                                  lambda b, h, i, j: (h, 0, jnp.clip(i - j, 0, 2), 0, 0)),
                     pl.BlockSpec((tq, tk), lambda b, h, i, j: (b * nq + i, jnp.minimum(i, j))),
                     pl.BlockSpec((tq, LANES), lambda b, h, i, j: (b * nq + i, 0))]
    return pl.pallas_call(
        functools.partial(_flash_kernel, mode=mode, tq=tq, tk=tk),
        grid=(nseq, N_KV, nq, nq),
        in_specs=in_specs,
        out_specs=pl.BlockSpec((tq, 256), lambda b, h, i, j: (b * nq + i, h)),
        out_shape=jax.ShapeDtypeStruct((nseq * t, N_HEADS * HEAD_DIM), F32),
        scratch_shapes=[pltpu.VMEM((2, tq, LANES), F32), pltpu.VMEM((2, tq, 2 * LANES), F32)],
        compiler_params=_cparams("parallel", "parallel", "parallel", "arbitrary"),
        name="flash_" + mode,
    )(pc, pc, pc, *extra)


def _split2(x):
    hi = x.astype(BF16)
    return hi, (x - hi.astype(F32)).astype(BF16)


def _sb_kernel(q_ref, k_ref, v_ref, o_ref, r_ref, acc_ref, *, tq, tk):
    qi = pl.program_id(2)
    ki = pl.program_id(3)

    @pl.when(ki == 0)
    def _():
        r_ref[...] = jnp.zeros_like(r_ref)
        acc_ref[...] = jnp.zeros_like(acc_ref)

    def step(diagonal):
        if diagonal:
            mask = (lax.broadcasted_iota(jnp.int32, (tq, tk), 1) < lax.broadcasted_iota(jnp.int32, (tq, tk), 0))
        after = jnp.concatenate([_tri(LANES, lambda r, c: r > c), jnp.ones((LANES, LANES), BF16)], axis=1)
        kb = k_ref[...].astype(BF16)
        vb = v_ref[...].astype(BF16)
        nblk = tk // LANES
        for g in range(2):
            q = q_ref[:, g * HEAD_DIM:(g + 1) * HEAD_DIM].astype(BF16)
            z = _dot_nt(q, kb) * HEAD_DIM ** -0.5
            sp = jnp.log(1.0 + jnp.exp(-jnp.abs(z)))
            log_take = jnp.minimum(z, 0.0) - sp
            log_stay = jnp.minimum(-z, 0.0) - sp
            if diagonal:
                log_stay = jnp.where(mask, log_stay, 0.0)
            carry = r_ref[g]
            later = [None] * nblk
            for c in reversed(range(nblk)):
                hi, lo = _split2(log_stay[:, c * LANES:(c + 1) * LANES])
                sums = _dot(hi, after) + _dot(lo, after)
                later[c] = sums[:, 0:LANES] + carry
                carry = carry + sums[:, LANES:2 * LANES]
            a = jnp.exp(log_take + jnp.concatenate(later, axis=1))
            if diagonal:
                a = jnp.where(mask, a, 0.0)
            acc_ref[g] += _dot(a.astype(BF16), vb)
            r_ref[g] = carry

    @pl.when(ki == 0)
    def _():
        step(True)

    @pl.when((ki > 0) & (ki <= qi))
    def _():
        step(False)

    @pl.when(ki == qi)
    def _():
        for g in range(2):
            o_ref[:, g * HEAD_DIM:(g + 1) * HEAD_DIM] = acc_ref[g]


def _sb_attention(pc, nseq, t, tq=ATTN_TILE):
    tk = tq
    nq = t // tq
    qb, kb, vb = C_SB_Q // 256, C_SB_K // HEAD_DIM, C_SB_V // HEAD_DIM
    return pl.pallas_call(
        functools.partial(_sb_kernel, tq=tq, tk=tk),
        grid=(nseq, N_KV, nq, nq),
        in_specs=[pl.BlockSpec((tq, 256), lambda b, h, i, j: (b * nq + i, qb + h)),
                  pl.BlockSpec((tk, HEAD_DIM), lambda b, h, i, j: (b * nq + jnp.maximum(i - j, 0), kb + h)),
                  pl.BlockSpec((tk, HEAD_DIM), lambda b, h, i, j: (b * nq + jnp.maximum(i - j, 0), vb + h))],
        out_specs=pl.BlockSpec((tq, 256), lambda b, h, i, j: (b * nq + i, h)),
        out_shape=jax.ShapeDtypeStruct((nseq * t, N_HEADS * HEAD_DIM), F32),
        scratch_shapes=[pltpu.VMEM((2, tq, LANES), F32), pltpu.VMEM((2, tq, HEAD_DIM), F32)],
        compiler_params=_cparams("parallel", "parallel", "parallel", "arbitrary"),
        name="sb_attention",
    )(pc, pc, pc)


KEY_NEG_INF = -2139095041


def _float_key(x):
    b = pltpu.bitcast(x, jnp.int32)
    return b ^ ((b >> 31) & 0x7FFFFFFF)


def _key_float(k):
    return pltpu.bitcast(k ^ ((k >> 31) & 0x7FFFFFFF), F32)


def _kth_largest_key(count_ge, rows, k):
    def body(i, res):
        bit = 31 - i
        cand = jnp.where(bit == 31, jnp.zeros_like(res), res | jnp.left_shift(1, jnp.minimum(bit, 30)))
        return jnp.where(count_ge(cand) >= k, cand, res)

    res = jnp.full((rows, 1), jnp.iinfo(jnp.int32).min, jnp.int32)
    return lax.fori_loop(0, 32, body, res)


def _tie_cut(tied_before, rows, need, nbits):
    def body(i, c):
        cand = c | jnp.left_shift(1, nbits - i)
        return jnp.where(tied_before(cand) < need, cand, c)

    return lax.fori_loop(0, nbits + 1, body, jnp.zeros((rows, 1), jnp.int32))


def _dsa_score_kernel(qi_ref, ki_ref, w_ref, sc_ref, thr_ref, key_ref, wrep_ref, *, tq, tk, topk):
    qi = pl.program_id(1)
    ki = pl.program_id(2)

    @pl.when((qi == 0) & (ki == 0))
    def _():
        key_ref[...] = jnp.full(key_ref.shape, jnp.iinfo(jnp.int32).min, jnp.int32)

    @pl.when(ki == 0)
    def _():
        hi, mid, lo = _split3(w_ref[...] * (IDX_DIM ** -0.5 * IDX_HEADS ** -0.5))
        r = lax.broadcasted_iota(jnp.int32, (LANES, LANES), 0)
        for h in range(IDX_HEADS):
            pick = jnp.where(r == S_DSA_W + h, 1.0, 0.0).astype(BF16)
            wrep_ref[h] = _dot(hi, pick) + _dot(mid, pick) + _dot(lo, pick)

    @pl.when(ki <= qi)
    def _():
        kidx = ki_ref[...].astype(BF16)
        nblk = tk // LANES
        blocks = [jnp.zeros((tq, LANES), F32)] * nblk
        for h in range(IDX_HEADS):
            q = qi_ref[:, h * IDX_DIM:(h + 1) * IDX_DIM].astype(BF16)
            si = _dot_nt(q, kidx)
            w = wrep_ref[h]
            blocks = [blk + jnp.maximum(si[:, c * LANES:(c + 1) * LANES], 0.0) * w for c, blk in enumerate(blocks)]
        score = jnp.concatenate(blocks, axis=1)
        t_pos = qi * tq + lax.broadcasted_iota(jnp.int32, (tq, tk), 0)
        s_pos = ki * tk + lax.broadcasted_iota(jnp.int32, (tq, tk), 1)
        score = jnp.where(s_pos <= t_pos, score, NEG_INF)
        key_ref[:, pl.ds(pl.multiple_of(ki * tk, tk), tk)] = _float_key(score)

    def search(width):
        def count_ge(cand):
            parts = []
            for r0 in range(0, tq, LANES):
                cand_r = cand[r0:r0 + LANES, :]
                cnt = jnp.zeros((LANES, LANES), jnp.int32)
                for c0 in range(0, width, LANES):
                    cnt = cnt + jnp.where(key_ref[r0:r0 + LANES, c0:c0 + LANES] >= cand_r, 1, 0)
                parts.append(jnp.sum(cnt, axis=-1, keepdims=True))
            return jnp.concatenate(parts, axis=0)

        res = jnp.maximum(_kth_largest_key(count_ge, tq, topk), KEY_NEG_INF)
        thr_ref[...] = jnp.broadcast_to(_key_float(res), thr_ref.shape)

        cnt_gt = count_ge(res + 1)
        need = topk - cnt_gt
        surplus = ((count_ge(res) - cnt_gt) > need) & (res > KEY_NEG_INF)

        @pl.when(jnp.max(jnp.where(surplus, 1, 0)) > 0)
        def _():
            pos = lax.broadcasted_iota(jnp.int32, (tq, width), 1)
            tied = key_ref[:, 0:width] == res

            def tied_before(c):
                return jnp.sum(jnp.where(tied & (pos < c), 1, 0), axis=-1, keepdims=True)

            cut = _tie_cut(tied_before, tq, jnp.maximum(need, 1), width.bit_length())
            drop = tied & (pos > cut) & surplus
            key_ref[:, 0:width] = jnp.where(drop, res - 1, key_ref[:, 0:width])

    total = key_ref.shape[1]
    widths = sorted({max(tk, (total * n // 4) // tk * tk) for n in (1, 2, 3, 4)})
    for n, width in enumerate(widths):
        lo = widths[n - 1] if n else 0

        @pl.when((ki == qi) & ((qi + 1) * tk > lo) & ((qi + 1) * tk <= width))
        def _(width=width):
            search(width)

    @pl.when(ki == qi)
    def _():
        keys = key_ref[...]
        sc_ref[...] = jnp.where(keys < KEY_NEG_INF, NEG_INF, _key_float(keys))


def _dsa_scores(pc, nseq, t, topk, tq=256):
    tk = tq
    nq = t // tq
    return pl.pallas_call(
        functools.partial(_dsa_score_kernel, tq=tq, tk=tk, topk=topk),
        grid=(nseq, nq, nq),
        in_specs=[pl.BlockSpec((tq, IDX_HEADS * IDX_DIM), lambda b, i, j: (b * nq + i, C_DSA_QIDX // 2048)),
                  pl.BlockSpec((tk, IDX_DIM), lambda b, i, j: (b * nq + jnp.minimum(i, j), C_DSA_KIDX // IDX_DIM)),
                  pl.BlockSpec((tq, LANES), lambda b, i, j: (b * nq + i, C_SMALL // LANES))],
        out_specs=[pl.BlockSpec((tq, t), lambda b, i, j: (b * nq + i, 0)),
                   pl.BlockSpec((tq, LANES), lambda b, i, j: (b * nq + i, 0))],
        out_shape=[jax.ShapeDtypeStruct((nseq * t, t), F32), jax.ShapeDtypeStruct((nseq * t, LANES), F32)],
        scratch_shapes=[pltpu.VMEM((tq, t), jnp.int32), pltpu.VMEM((IDX_HEADS, tq, LANES), F32)],
        compiler_params=_cparams("arbitrary", "arbitrary", "arbitrary"),
        name="dsa_scores",
    )(pc, pc, pc)


def _t5_bucket(rel):
    n = jnp.maximum(rel, 0)
    max_exact = N_BUCKETS // 2
    nf = jnp.maximum(n, 1).astype(F32)
    large = max_exact + (jnp.log(nf / max_exact) / math.log(MAX_DISTANCE / max_exact)
                         * (N_BUCKETS - max_exact)).astype(jnp.int32)
    large = jnp.minimum(large, N_BUCKETS - 1)
    return jnp.where(n < max_exact, n, large)


def _rel_bias_tiles(rel_bias, tq):
    i = jnp.arange(tq, dtype=jnp.int32)[:, None]
    j = jnp.arange(tq, dtype=jnp.int32)[None, :]
    rel = jnp.stack([i - j, tq + i - j, 2 * tq + i - j])
    return _bucket_bias(rel_bias, _t5_bucket(rel)).reshape(N_KV, 2, 3, tq, tq)


def _bucket_bias(rel_bias, bucket):
    table = rel_bias.astype(F32)
    out = jnp.zeros((table.shape[1],) + bucket.shape, F32)
    for b in range(table.shape[0]):
        out = jnp.where(bucket[None] == b, table[b].reshape((-1,) + (1,) * bucket.ndim), out)
    return out


GDN_W = N_HEADS * HEAD_DIM


def _softplus(x):
    return jnp.maximum(x, 0.0) + jnp.log(1.0 + jnp.exp(-jnp.abs(x)))


def _l2norm_heads(x, scale):
    parts = []
    for h in range(N_HEADS):
        xh = x[:, h * HEAD_DIM:(h + 1) * HEAD_DIM]
        parts.append(xh * (lax.rsqrt(jnp.sum(xh * xh, axis=-1, keepdims=True) + EPS) * scale))
    return jnp.concatenate(parts, axis=-1)


def _gdn_gates(small, alog_row, dt_row):
    glog = -jnp.exp(alog_row) * _softplus(small + dt_row)
    beta = 1.0 / (1.0 + jnp.exp(-small))
    lane = lax.broadcasted_iota(jnp.int32, (small.shape[0], LANES), 1)
    blocks = []
    for h in range(N_HEADS):
        g_col = glog[:, S_GDN_A + h:S_GDN_A + h + 1]
        b_col = beta[:, S_GDN_B + h:S_GDN_B + h + 1]
        blocks.append(jnp.where(lane == 0, g_col, jnp.where(lane == 1, b_col, 0.0)))
    return jnp.concatenate(blocks, axis=-1)


def _gdn_prep_prompt_kernel(uq_ref, uk_ref, uv_ref, cw_ref, small_ref, alog_ref, dt_ref,
                            q_ref, k_ref, v_ref, gb_ref, sq_ref, sk_ref, sv_ref, carry_ref):
    @pl.when(pl.program_id(1) == 0)
    def _():
        carry_ref[...] = jnp.zeros_like(carry_ref)

    def conv(idx, u_ref, s_ref):
        u = u_ref[...]
        prev = carry_ref[idx]
        cw = cw_ref[:, idx * GDN_W:(idx + 1) * GDN_W]
        out = cw[GDN_CONV - 1:GDN_CONV, :] * u
        for j in range(GDN_CONV - 1):
            out = out + cw[j:j + 1, :] * _shift_rows(u, prev, GDN_CONV - 1 - j)
        last = u[u.shape[0] - SUBLANES:, :]
        carry_ref[idx] = last
        s_ref[...] = last
        return _silu(out)

    q_ref[...] = _l2norm_heads(conv(0, uq_ref, sq_ref), HEAD_DIM ** -0.5)
    k_ref[...] = _l2norm_heads(conv(1, uk_ref, sk_ref), 1.0)
    v_ref[...] = conv(2, uv_ref, sv_ref)
    gb_ref[...] = _gdn_gates(small_ref[...], alog_ref[...], dt_ref[...])


def _gdn_prep_prompt(pc, conv_w, alog_row, dt_row, nseq, t, tm=256):
    nt = t // tm
    ub = C_GDN_QKV // GDN_W
    row_spec = pl.BlockSpec((tm, GDN_W), lambda b, i: (b * nt + i, 0))
    st_spec = pl.BlockSpec((None, SUBLANES, GDN_W), lambda b, i: (b, 0, 0))
    par_spec = pl.BlockSpec((1, LANES), lambda b, i: (0, 0))
    outs = pl.pallas_call(
        _gdn_prep_prompt_kernel,
        grid=(nseq, nt),
        in_specs=[pl.BlockSpec((tm, GDN_W), lambda b, i: (b * nt + i, ub)),
                  pl.BlockSpec((tm, GDN_W), lambda b, i: (b * nt + i, ub + 1)),
                  pl.BlockSpec((tm, GDN_W), lambda b, i: (b * nt + i, ub + 2)),
                  pl.BlockSpec((GDN_CONV, 3 * GDN_W), lambda b, i: (0, 0)),
                  pl.BlockSpec((tm, LANES), lambda b, i: (b * nt + i, C_SMALL // LANES)),
                  par_spec, par_spec],
        out_specs=[row_spec, row_spec, row_spec, row_spec, st_spec, st_spec, st_spec],
        out_shape=[jax.ShapeDtypeStruct((nseq * t, GDN_W), F32)] * 4
        + [jax.ShapeDtypeStruct((nseq, SUBLANES, GDN_W), F32)] * 3,
        scratch_shapes=[pltpu.VMEM((3, SUBLANES, GDN_W), F32)],
        compiler_params=_cparams("parallel", "arbitrary"),
        name="gdn_prep_prompt",
    )(pc, pc, pc, conv_w, pc, alog_row, dt_row)
    q, k, v, gb, sq, sk, sv = outs
    conv_state = jnp.concatenate([sq, sk, sv], axis=-1)[:, SUBLANES - (GDN_CONV - 1):, :]
    return q, k, v, gb, conv_state


def _dot_3pass(a, b):
    a_hi, a_lo = _split2(a)
    b_hi, b_lo = _split2(b)
    return _dot(a_hi, b_hi) + (_dot(a_hi, b_lo) + _dot(a_lo, b_hi))


def _gdn_chunk_kernel(q_ref, k_ref, v_ref, gb_ref, z_ref, nw_ref, o_ref, sfin_ref, s_ref, *, chunks, heads):
    c = GDN_CHUNK

    @pl.when(pl.program_id(2) == 0)
    def _():
        s_ref[...] = jnp.zeros_like(s_ref)

    n = chunks * c
    r = lax.broadcasted_iota(jnp.int32, (n, n), 0)
    col = lax.broadcasted_iota(jnp.int32, (n, n), 1)
    same = (r // c) == (col // c)
    lower = same & (r >= col)
    strict = same & (r > col)
    eye = jnp.where(r == col, 1.0, 0.0)
    low01 = jnp.where(lower, 1.0, 0.0).astype(BF16)
    up01 = jnp.where(same & (col > r), 1.0, 0.0).astype(BF16)

    def chunk_terms(hh):
        cols = slice(hh * HEAD_DIM, (hh + 1) * HEAD_DIM)
        q = q_ref[:, cols]
        k = k_ref[:, cols]
        v = v_ref[:, cols]
        g = gb_ref[:, hh * HEAD_DIM:hh * HEAD_DIM + 1]
        beta = gb_ref[:, hh * HEAD_DIM + 1:hh * HEAD_DIM + 2]
        g_lanes = jnp.broadcast_to(g, (n, HEAD_DIM))
        gc = _dot_exact01(low01, g_lanes)
        rest = _dot_exact01(up01, g_lanes)
        diff = _dot_exact01(low01, jnp.where(strict, jnp.broadcast_to(g, (n, n)), 0.0))
        decay = jnp.where(lower, jnp.exp(jnp.where(lower, diff, 0.0)), 0.0)
        kb = k * beta
        lmat = jnp.where(strict, _dot_nt(kb.astype(BF16), k.astype(BF16)) * decay, 0.0)
        p = -lmat
        tinv = eye + p
        for _ in range(5):
            p = _dot_3pass(p, p)
            tinv = tinv + _dot_3pass(tinv, p)
        tb = tinv.astype(BF16)
        u = _dot(tb, (v * beta).astype(BF16))
        w = _dot(tb, (kb * jnp.exp(gc)).astype(BF16)).astype(BF16)
        aqk = jnp.where(lower, _dot_nt(q.astype(BF16), k.astype(BF16)) * decay, 0.0).astype(BF16)
        q_dec = (q * jnp.exp(gc)).astype(BF16)
        k_dec = (k * jnp.exp(rest)).astype(BF16)
        return u, w, aqk, q_dec, k_dec, jnp.exp(gc)

    terms = [chunk_terms(hh) for hh in range(heads)]
    states = [s_ref[hh] for hh in range(heads)]
    for ci in range(chunks):
        rows = slice(ci * c, (ci + 1) * c)
        for hh in range(heads):
            cols = slice(hh * HEAD_DIM, (hh + 1) * HEAD_DIM)
            u, w, aqk, q_dec, k_dec, chunk_decay = terms[hh]
            s = states[hh]
            sb = s.astype(BF16)
            v_new = u[rows, :] - _dot(w[rows, :], sb)
            o = _dot(q_dec[rows, :], sb) + _dot(aqk[rows, rows], v_new.astype(BF16))
            states[hh] = (s * chunk_decay[(ci + 1) * c - 1:(ci + 1) * c, :]
                          + _dot_tn(k_dec[rows, :], v_new.astype(BF16)))
            o_ref[rows, cols] = _rms(o, nw_ref[...]) * _silu(z_ref[rows, cols])
    for hh in range(heads):
        s_ref[hh] = states[hh]

    @pl.when(pl.program_id(2) == pl.num_programs(2) - 1)
    def _():
        sfin_ref[...] = s_ref[...]


def _gdn_chunked(q, k, v, gb, pc, norm_w, nseq, t, tm=256, heads=2):
    nt = t // tm
    width = heads * HEAD_DIM
    zb = C_GDN_Z // width
    head_spec = pl.BlockSpec((tm, width), lambda b, h, i: (b * nt + i, h))
    return pl.pallas_call(
        functools.partial(_gdn_chunk_kernel, chunks=tm // GDN_CHUNK, heads=heads),
        grid=(nseq, N_HEADS // heads, nt),
        in_specs=[head_spec, head_spec, head_spec, head_spec,
                  pl.BlockSpec((tm, width), lambda b, h, i: (b * nt + i, zb + h)),
                  pl.BlockSpec((1, HEAD_DIM), lambda b, h, i: (0, 0))],
        out_specs=[head_spec,
                   pl.BlockSpec((None, heads, HEAD_DIM, HEAD_DIM), lambda b, h, i: (b, h, 0, 0))],
        out_shape=[jax.ShapeDtypeStruct((nseq * t, GDN_W), F32),
                   jax.ShapeDtypeStruct((nseq, N_HEADS, HEAD_DIM, HEAD_DIM), F32)],
        scratch_shapes=[pltpu.VMEM((heads, HEAD_DIM, HEAD_DIM), F32)],
        compiler_params=_cparams("parallel", "parallel", "arbitrary"),
        name="gdn_chunked",
    )(q, k, v, gb, pc, norm_w.reshape(1, HEAD_DIM))


def _gdn_prep_sample_kernel(*refs):
    u_refs = refs[0:3]
    p_refs = refs[3:12]
    cw_ref, small_ref, alog_ref, dt_ref, bfox_ref = refs[12:17]
    q_ref, k_ref, v_ref, gb_ref, lf_ref = refs[17:22]

    def conv(idx):
        cw = cw_ref[:, idx * GDN_W:(idx + 1) * GDN_W]
        out = cw[GDN_CONV - 1:GDN_CONV, :] * u_refs[idx][...]
        for r in range(GDN_CONV - 1):
            out = out + cw[r:r + 1, :] * p_refs[3 * r + idx][...]
        return _silu(out)

    q_ref[...] = _l2norm_heads(conv(0), HEAD_DIM ** -0.5)
    k_ref[...] = _l2norm_heads(conv(1), 1.0)
    v_ref[...] = conv(2)
    gb_ref[...] = _gdn_gates(small_ref[...], alog_ref[...], dt_ref[...])
    lf_ref[...] = _log_sigmoid(small_ref[...] + bfox_ref[...])


def _gdn_prep_sample(pc, prev, conv_w, alog_row, dt_row, bfox_row):
    m = pc.shape[0]
    ub = C_GDN_QKV // GDN_W
    prev2 = prev.reshape(m, (GDN_CONV - 1) * 3 * GDN_W)
    row_spec = pl.BlockSpec((m, GDN_W), lambda i: (0, 0))
    par_spec = pl.BlockSpec((1, LANES), lambda i: (0, 0))
    in_specs = [pl.BlockSpec((m, GDN_W), functools.partial(lambda i, c: (0, c), c=ub + n)) for n in range(3)]
    in_specs += [pl.BlockSpec((m, GDN_W), functools.partial(lambda i, c: (0, c), c=n)) for n in range(9)]
    in_specs += [pl.BlockSpec((GDN_CONV, 3 * GDN_W), lambda i: (0, 0)),
                 pl.BlockSpec((m, LANES), lambda i: (0, C_SMALL // LANES)), par_spec, par_spec, par_spec]
    return pl.pallas_call(
        _gdn_prep_sample_kernel,
        grid=(1,),
        in_specs=in_specs,
        out_specs=[row_spec] * 4 + [pl.BlockSpec((m, LANES), lambda i: (0, 0))],
        out_shape=[jax.ShapeDtypeStruct((m, GDN_W), F32)] * 4 + [jax.ShapeDtypeStruct((m, LANES), F32)],
        compiler_params=_cparams("arbitrary"),
        name="gdn_prep_sample",
    )(pc, pc, pc, *([prev2] * 9), conv_w, pc, alog_row, dt_row, bfox_row)


def _gdn_step_kernel(q_ref, k_ref, v_ref, gb_ref, z_ref, nw_ref, s_ref, o_ref, so_ref, *, tb):
    pad = jnp.zeros((HEAD_DIM - tb, HEAD_DIM), F32)
    for h in range(N_HEADS):
        cols = slice(h * HEAD_DIM, (h + 1) * HEAD_DIM)
        k_t = jnp.concatenate([k_ref[:, cols], pad], axis=0).T
        q_t = jnp.concatenate([q_ref[:, cols], pad], axis=0).T
        for b in range(tb):
            decay = jnp.exp(gb_ref[b:b + 1, h * HEAD_DIM:h * HEAD_DIM + 1])
            beta = gb_ref[b:b + 1, h * HEAD_DIM + 1:h * HEAD_DIM + 2]
            kc = k_t[:, b:b + 1]
            s = s_ref[b, h] * decay
            v_old = jnp.sum(s * kc, axis=0, keepdims=True)
            s = s + kc * ((v_ref[b:b + 1, cols] - v_old) * beta)
            so_ref[b, h] = s
            o = jnp.sum(s * q_t[:, b:b + 1], axis=0, keepdims=True)
            o_ref[b:b + 1, cols] = _rms(o, nw_ref[...]) * _silu(z_ref[b:b + 1, cols])


def _gdn_step(q, k, v, gb, pc, norm_w, state, tb=8):
    m = q.shape[0]
    row_spec = pl.BlockSpec((tb, GDN_W), lambda i: (i, 0))
    st_spec = pl.BlockSpec((tb, N_HEADS, HEAD_DIM, HEAD_DIM), lambda i: (i, 0, 0, 0))
    return pl.pallas_call(
        functools.partial(_gdn_step_kernel, tb=tb),
        grid=(m // tb,),
        in_specs=[row_spec, row_spec, row_spec, row_spec,
                  pl.BlockSpec((tb, GDN_W), lambda i: (i, C_GDN_Z // GDN_W)),
                  pl.BlockSpec((1, HEAD_DIM), lambda i: (0, 0)), st_spec],
        out_specs=[row_spec, st_spec],
        out_shape=[jax.ShapeDtypeStruct((m, GDN_W), F32), jax.ShapeDtypeStruct(state.shape, F32)],
        compiler_params=_cparams("parallel"),
        name="gdn_step",
    )(q, k, v, gb, pc, norm_w.reshape(1, HEAD_DIM), state)


def _dec_idx_kernel(pt_ref, q_ref, w_ref, knew_ref, *refs, page):
    kidx_refs, (sc_ref, self_ref) = refs[:-2], refs[-2:]
    q = q_ref[...]
    qb = q.astype(BF16)
    w = w_ref[...] * (IDX_DIM ** -0.5 * IDX_HEADS ** -0.5)
    for pg, kidx_ref in enumerate(kidx_refs):
        si = _dot_nt(qb, kidx_ref[...].astype(BF16))
        sc_ref[:, pg * page:(pg + 1) * page] = jnp.sum(jnp.maximum(si, 0.0) * w, axis=0, keepdims=True)
    si_new = jnp.sum(q * knew_ref[...], axis=-1, keepdims=True)
    own = jnp.sum(jnp.maximum(si_new, 0.0) * w, axis=0, keepdims=True)
    self_ref[...] = jnp.broadcast_to(own, self_ref.shape)


def _dec_idx_scores(layer, page_table, qidx3, w3, knew3, cache_kidx):
    nb, npages = page_table.shape
    page = cache_kidx.shape[2]
    page_specs = [pl.BlockSpec((None, None, page, IDX_DIM),
                               functools.partial(lambda b, pt, pg: (layer, pt[b, pg], 0, 0), pg=pg))
                  for pg in range(npages)]
    grid_spec = pltpu.PrefetchScalarGridSpec(
        num_scalar_prefetch=1,
        grid=(nb,),
        in_specs=[pl.BlockSpec((None, IDX_HEADS, IDX_DIM), lambda b, pt: (b, 0, 0)),
                  pl.BlockSpec((None, IDX_HEADS, 1), lambda b, pt: (b, 0, 0)),
                  pl.BlockSpec((None, 1, IDX_DIM), lambda b, pt: (b, 0, 0))] + page_specs,
        out_specs=[pl.BlockSpec((None, 1, npages * page), lambda b, pt: (b, 0, 0)),
                   pl.BlockSpec((None, 1, LANES), lambda b, pt: (b, 0, 0))])
    return pl.pallas_call(
        functools.partial(_dec_idx_kernel, page=page),
        grid_spec=grid_spec,
        out_shape=[jax.ShapeDtypeStruct((nb, 1, npages * page), F32), jax.ShapeDtypeStruct((nb, 1, LANES), F32)],
        compiler_params=_cparams("parallel"),
        name="dec_idx_scores",
    )(page_table, qidx3, w3, knew3, *([cache_kidx] * npages))


def _dec_thr_kernel(sc_ref, self_ref, sel_ref, own_sel_ref, *, topk):
    scores = sc_ref[...]
    keys = _float_key(scores)
    own = _float_key(self_ref[:, 0:1])

    def count_ge(cand):
        cnt = jnp.sum(jnp.where(keys >= cand, 1, 0), axis=-1, keepdims=True)
        return cnt + jnp.where(own >= cand, 1, 0)

    nb, n = keys.shape
    res = jnp.maximum(_kth_largest_key(count_ge, nb, topk), KEY_NEG_INF)
    need = topk - count_ge(res + 1)
    pos = lax.broadcasted_iota(jnp.int32, (nb, n), 1)
    tied = keys == res

    def tied_before(c):
        return jnp.sum(jnp.where(tied & (pos < c), 1, 0), axis=-1, keepdims=True)

    cut = _tie_cut(tied_before, nb, need, n.bit_length())
    keep = (keys > res) | (tied & (pos <= cut))
    own_keep = (own > res) | ((own == res) & (tied_before(n) < need))
    own_sel_ref[...] = jnp.broadcast_to(jnp.where(own_keep, 1.0, 0.0), own_sel_ref.shape)
    r = lax.broadcasted_iota(jnp.int32, (LANES, N_KV * LANES), 0)
    c = lax.broadcasted_iota(jnp.int32, (LANES, N_KV * LANES), 1)
    spread = jnp.where(r == c // N_KV, 1.0, 0.0).astype(BF16)
    for j in range(n // LANES):
        sel = jnp.where(keep[:, j * LANES:(j + 1) * LANES], 1.0, 0.0).astype(BF16)
        sel_ref[:, j * N_KV * LANES:(j + 1) * N_KV * LANES] = _dot(sel, spread)


def _dec_threshold(scores, own, topk):
    nb, n = scores.shape
    return pl.pallas_call(
        functools.partial(_dec_thr_kernel, topk=topk),
        grid=(1,),
        in_specs=[pl.BlockSpec((nb, n), lambda i: (0, 0)), pl.BlockSpec((nb, LANES), lambda i: (0, 0))],
        out_specs=[pl.BlockSpec((nb, N_KV * n), lambda i: (0, 0)), pl.BlockSpec((nb, LANES), lambda i: (0, 0))],
        out_shape=[jax.ShapeDtypeStruct((nb, N_KV * n), F32), jax.ShapeDtypeStruct((nb, LANES), F32)],
        compiler_params=_cparams("arbitrary"),
        name="dec_threshold",
    )(scores, own)


N_CACHES = 6


def _dec_attn_kernel(pt_ref, fq_ref, sq_ref, dq_ref, fkn_ref, fvn_ref, dkn_ref, dvn_ref, lfn_ref, sel_ref, own_ref,
                     bias0_ref, bias_ref, fk_hbm, fv_hbm, sk_hbm, sv_hbm, dk_hbm, dv_hbm, lf_hbm,
                     of_ref, os_ref, od_ref, kv_buf, lf_buf, sems, *, layer, npages, page):
    b = pl.program_id(0)
    nb = pl.num_programs(0)
    rows = N_KV * page
    n = npages * rows
    caches = (fk_hbm, fv_hbm, sk_hbm, sv_hbm, dk_hbm, dv_hbm)
    scale = HEAD_DIM ** -0.5

    def page_copies(seq, slot):
        out = []
        for pg in range(npages):
            pid = pt_ref[seq, pg]
            for c, hbm in enumerate(caches):
                out.append(pltpu.make_async_copy(hbm.at[layer, pid], kv_buf.at[slot, c, pl.ds(pg * rows, rows)],
                                                 sems.at[slot, c]))
            out.append(pltpu.make_async_copy(lf_hbm.at[layer, pid], lf_buf.at[slot, pg], sems.at[slot, N_CACHES]))
        return out

    slot = b % 2

    @pl.when(b == 0)
    def _():
        for cp in page_copies(0, 0):
            cp.start()

    @pl.when(b + 1 < nb)
    def _():
        for cp in page_copies(b + 1, 1 - slot):
            cp.start()

    for cp in page_copies(b, slot):
        cp.wait()

    row = lax.broadcasted_iota(jnp.int32, (SUBLANES, n), 0)
    col = lax.broadcasted_iota(jnp.int32, (SUBLANES, n), 1)
    mine = (col % N_KV) == jnp.where(row < 2, 0, 1)
    top = lax.broadcasted_iota(jnp.int32, (SUBLANES, HEAD_DIM), 0) < 2

    def per_head(x2):
        return jnp.where(top, x2[0:1, :], x2[1:2, :])

    def slab(c):
        return kv_buf[slot, c].astype(BF16)

    rr = lax.broadcasted_iota(jnp.int32, (page, rows), 0)
    cc = lax.broadcasted_iota(jnp.int32, (page, rows), 1)
    pos_after = jnp.where(rr > cc // N_KV, 1.0, 0.0).astype(BF16)
    carry = lfn_ref[:, 0:1]
    decay = [None] * npages
    for pg in reversed(range(npages)):
        lf = lf_buf[slot, pg]
        hi, mid, lo = _split3(lf)
        decay[pg] = _dot(hi, pos_after) + _dot(mid, pos_after) + _dot(lo, pos_after) + carry
        carry = carry + jnp.sum(lf, axis=-1, keepdims=True)
    s = _dot_nt(fq_ref[...].astype(BF16), slab(0)) * scale + jnp.concatenate(decay, axis=-1)
    s = jnp.where(mine, s, MASKED)
    s_own = jnp.sum(fq_ref[...] * per_head(fkn_ref[...]), axis=-1, keepdims=True) * scale
    m = jnp.maximum(jnp.max(s, axis=-1, keepdims=True), s_own)
    w = jnp.where(mine, jnp.exp(s - m), 0.0)
    w_own = jnp.exp(s_own - m)
    acc = _dot(w.astype(BF16), slab(1)) + w_own * per_head(fvn_ref[...])
    of_ref[...] = acc / (jnp.sum(w, axis=-1, keepdims=True) + w_own)

    z = _dot_nt(sq_ref[...].astype(BF16), slab(2)) * scale
    sp = jnp.log(1.0 + jnp.exp(-jnp.abs(z)))
    log_take = jnp.minimum(z, 0.0) - sp
    log_stay = jnp.where(mine, jnp.minimum(-z, 0.0) - sp, 0.0)
    r2 = lax.broadcasted_iota(jnp.int32, (rows, rows), 0)
    c2 = lax.broadcasted_iota(jnp.int32, (rows, rows), 1)
    row_after = jnp.where(r2 // N_KV > c2 // N_KV, 1.0, 0.0).astype(BF16)
    carry = jnp.zeros((SUBLANES, 1), F32)
    later = [None] * npages
    for pg in reversed(range(npages)):
        ls = log_stay[:, pg * rows:(pg + 1) * rows]
        hi, lo = _split2(ls)
        later[pg] = _dot(hi, row_after) + _dot(lo, row_after) + carry
        carry = carry + jnp.sum(ls, axis=-1, keepdims=True)
    a = jnp.where(mine, jnp.exp(log_take + jnp.concatenate(later, axis=-1)), 0.0)
    os_ref[...] = _dot(a.astype(BF16), slab(3))

    keep = mine & (sel_ref[...] > 0.5)
    s = jnp.where(keep, _dot_nt(dq_ref[...].astype(BF16), slab(4)) * scale + bias_ref[...], MASKED)
    own_on = own_ref[:, 0:1] > 0.5
    d_own = jnp.sum(dq_ref[...] * per_head(dkn_ref[...]), axis=-1, keepdims=True) * scale + bias0_ref[:, 0:1]
    d_own = jnp.where(own_on, d_own, MASKED)
    m = jnp.maximum(jnp.max(s, axis=-1, keepdims=True), d_own)
    w = jnp.where(keep, jnp.exp(s - m), 0.0)
    w_own = jnp.where(own_on, jnp.exp(d_own - m), 0.0)
    acc = _dot(w.astype(BF16), slab(5)) + w_own * per_head(dvn_ref[...])
    od_ref[...] = acc / (jnp.sum(w, axis=-1, keepdims=True) + w_own)


def _dec_attention(layer, page_table, q8, new_kv, lfn8, sel3, own3, bias0, bias_dec, caches, lf_cache8):
    nb, npages = page_table.shape
    page = caches[0].shape[2] // N_KV
    n = npages * page * N_KV
    per_b8 = pl.BlockSpec((None, SUBLANES, HEAD_DIM), lambda b, pt: (b, 0, 0))
    per_b2 = pl.BlockSpec((None, N_KV, HEAD_DIM), lambda b, pt: (b, 0, 0))
    hbm = pl.BlockSpec(memory_space=pl.ANY)
    in_specs = [per_b8, per_b8, per_b8, per_b2, per_b2, per_b2, per_b2, per_b8,
                pl.BlockSpec((None, 1, n), lambda b, pt: (b, 0, 0)),
                pl.BlockSpec((None, 1, LANES), lambda b, pt: (b, 0, 0)),
                pl.BlockSpec((SUBLANES, LANES), lambda b, pt: (0, 0)),
                pl.BlockSpec((SUBLANES, n), lambda b, pt: (0, 0))] + [hbm] * (N_CACHES + 1)
    grid_spec = pltpu.PrefetchScalarGridSpec(
        num_scalar_prefetch=1, grid=(nb,), in_specs=in_specs,
        out_specs=[per_b8, per_b8, per_b8],
        scratch_shapes=[pltpu.VMEM((2, N_CACHES, n, HEAD_DIM), F32),
                        pltpu.VMEM((2, npages, SUBLANES, page), F32),
                        pltpu.SemaphoreType.DMA((2, N_CACHES + 1))])
    return pl.pallas_call(
        functools.partial(_dec_attn_kernel, layer=layer, npages=npages, page=page),
        grid_spec=grid_spec,
        out_shape=[jax.ShapeDtypeStruct((nb, SUBLANES, HEAD_DIM), F32)] * 3,
        compiler_params=_cparams("arbitrary"),
        name="dec_attention",
    )(page_table, *q8, *new_kv, lfn8, sel3, own3, bias0, bias_dec, *caches, lf_cache8)


def _lane_row(values, offset):
    return jnp.zeros((1, LANES), F32).at[0, offset:offset + values.shape[0]].set(values.astype(F32))


def _layer_params(l, w_mod, b_mod, g_mix_pre, g_mix_post, g_ffn_pre, g_ffn_post, w_in, b_fox_f, gdn_a_log,
                  gdn_dt_bias, gdn_conv_w, gdn_norm_w, w_out, w_up, ffn_conv_w, ffn_conv_b, w_down):
    segments, n_real = _in_proj_permutation(w_in.shape[1])
    parts = [w_in[l][:, s:s + n].astype(BF16) for s, n in segments]
    w_in_p = jnp.concatenate(parts + [jnp.zeros((w_in.shape[1], D_IN_PAD - n_real), BF16)], axis=1)
    return dict(
        w_mod=w_mod[l], b_mod=b_mod[l], g_mix_pre=g_mix_pre[l], g_mix_post=g_mix_post[l],
        g_ffn_pre=g_ffn_pre[l], g_ffn_post=g_ffn_post[l], w_in=w_in_p,
        b_fox_row=_lane_row(b_fox_f[l], S_FOX_F), alog_row=_lane_row(gdn_a_log[l], S_GDN_A),
        dt_row=_lane_row(gdn_dt_bias[l], S_GDN_A), gdn_conv_w=gdn_conv_w[l], gdn_norm_w=gdn_norm_w[l],
        w_out=w_out[l].astype(BF16), w_up=w_up[l].astype(BF16), ffn_conv_w=ffn_conv_w[l],
        ffn_conv_b=ffn_conv_b[l], w_down=w_down[l].astype(BF16))


def _kv_state(pc, col, lead):
    return pc[:, col:col + N_KV * HEAD_DIM].reshape(*lead, N_KV, HEAD_DIM)


def _prompt_layer(x, mod, p, rel_tiles, nseq, t):
    pc = _in_proj(x, p['g_mix_pre'], mod, p['w_in'], t, tm=1024, tn=512)
    lf, cum, cum_rep = _fox_gates(pc, p['b_fox_row'], nseq, t)
    cum_rows = jnp.transpose(cum[:, :N_HEADS].reshape(nseq, t, N_KV, 2), (0, 2, 3, 1)).reshape(nseq * N_KV, 2, t)
    o_fox = _flash(pc, nseq, t, 'fox', (cum_rep, cum_rows))
    o_sb = _sb_attention(pc, nseq, t)
    gq, gk, gv, gb, gdn_conv = _gdn_prep_prompt(pc, p['gdn_conv_w'], p['alog_row'], p['dt_row'], nseq, t)
    o_gdn, gdn_s = _gdn_chunked(gq, gk, gv, gb, pc, p['gdn_norm_w'], nseq, t)
    scores, thr = _dsa_scores(pc, nseq, t, min(TOPK_MAX, t // 4))
    o_dsa = _flash(pc, nseq, t, 'dsa', (rel_tiles, scores, thr))
    x = _out_proj((o_fox, o_sb, o_gdn, o_dsa), p['w_out'], x, p['g_mix_post'], mod, t, tm=256)
    x, ffn_conv = _ffn_prompt(x, p['g_ffn_pre'], mod, p['w_up'], p['ffn_conv_w'], p['ffn_conv_b'], p['w_down'],
                              p['g_ffn_post'], t, tm=512, tf=512)
    lead = (nseq, t)
    states = (_kv_state(pc, C_FOX_K, lead), _kv_state(pc, C_FOX_V, lead), lf[:, :N_HEADS].reshape(nseq, t, N_HEADS),
              _kv_state(pc, C_SB_K, lead), _kv_state(pc, C_SB_V, lead),
              _kv_state(pc, C_DSA_K, lead), _kv_state(pc, C_DSA_V, lead),
              pc[:, C_DSA_KIDX:C_DSA_KIDX + IDX_DIM].reshape(nseq, t, IDX_DIM), gdn_s, gdn_conv, ffn_conv)
    return x, states


def _heads8(x):
    nb = x.shape[0]
    return jnp.pad(x.reshape(nb, N_HEADS, HEAD_DIM), ((0, 0), (0, SUBLANES - N_HEADS), (0, 0)))


def _sample_layer(layer, x, mod, p, caches, lf_cache8, kidx_cache, page_table, bias_dec, bias0,
                  gdn_s, gdn_conv, ffn_conv):
    nb = x.shape[0]
    pc = _in_proj(x, p['g_mix_pre'], mod, p['w_in'], 1, tm=nb, tn=512)

    def cols(c, n):
        return pc[:, c:c + n]

    gq, gk, gv, gb, lf = _gdn_prep_sample(pc, gdn_conv, p['gdn_conv_w'], p['alog_row'], p['dt_row'], p['b_fox_row'])
    o_gdn, gdn_s_new = _gdn_step(gq, gk, gv, gb, pc, p['gdn_norm_w'], gdn_s)
    gdn_conv_new = jnp.concatenate([gdn_conv[:, 1:], cols(C_GDN_QKV, 3 * GDN_W)[:, None]], axis=1)

    qidx3 = cols(C_DSA_QIDX, IDX_HEADS * IDX_DIM).reshape(nb, IDX_HEADS, IDX_DIM)
    w3 = cols(C_SMALL + S_DSA_W, IDX_HEADS).reshape(nb, IDX_HEADS, 1)
    knew3 = cols(C_DSA_KIDX, IDX_DIM).reshape(nb, 1, IDX_DIM)
    scores3, own3 = _dec_idx_scores(layer, page_table, qidx3, w3, knew3, kidx_cache)
    past = scores3.shape[-1]
    sel, own_sel = _dec_threshold(scores3.reshape(nb, past), own3.reshape(nb, LANES),
                                  min(TOPK_MAX, (past + 1) // 4))

    lf4 = lf[:, :N_HEADS]
    lfn8 = jnp.broadcast_to(jnp.pad(lf4, ((0, 0), (0, SUBLANES - N_HEADS)))[:, :, None], (nb, SUBLANES, LANES))
    q8 = (_heads8(cols(C_FOX_Q, 512)), _heads8(cols(C_SB_Q, 512)), _heads8(cols(C_DSA_Q, 512)))
    new_kv = tuple(cols(c, 256).reshape(nb, N_KV, HEAD_DIM) for c in (C_FOX_K, C_FOX_V, C_DSA_K, C_DSA_V))
    o8 = _dec_attention(layer, page_table, q8, new_kv, lfn8, sel.reshape(nb, 1, N_KV * past),
                        own_sel.reshape(nb, 1, LANES), bias0, bias_dec, caches, lf_cache8)
    o_fox, o_sb, o_dsa = (o[:, :N_HEADS].reshape(nb, N_HEADS * HEAD_DIM) for o in o8)

    x = _out_proj((o_fox, o_sb, o_gdn, o_dsa), p['w_out'], x, p['g_mix_post'], mod, 1, tm=nb)
    x, up = _ffn_sample(x, p['g_ffn_pre'], mod, p['w_up'], p['ffn_conv_w'], p['ffn_conv_b'], p['w_down'],
                        p['g_ffn_post'], ffn_conv, tf=512)
    ffn_conv_new = jnp.concatenate([ffn_conv[:, 1:], up[:, None]], axis=1)
    lead = (nb, 1)
    states = (_kv_state(pc, C_FOX_K, lead), _kv_state(pc, C_FOX_V, lead), lf4.reshape(nb, 1, N_HEADS),
              _kv_state(pc, C_SB_K, lead), _kv_state(pc, C_SB_V, lead),
              _kv_state(pc, C_DSA_K, lead), _kv_state(pc, C_DSA_V, lead),
              cols(C_DSA_KIDX, IDX_DIM).reshape(nb, 1, IDX_DIM), gdn_s_new, gdn_conv_new, ffn_conv_new)
    return x, states


def kernel(x_prompt, x_sample, cache_fox_k, cache_fox_v, cache_fox_logf, cache_sb_k, cache_sb_v, cache_dsa_k,
           cache_dsa_v, cache_dsa_kidx, state_gdn_s, state_gdn_conv, state_ffn_conv, page_table, c_prompt,
           c_sample, w_mod, b_mod, g_mix_pre, g_mix_post, g_ffn_pre, g_ffn_post, w_in, b_fox_f, gdn_a_log,
           gdn_dt_bias, gdn_conv_w, gdn_norm_w, rel_bias, w_out, w_up, ffn_conv_w, ffn_conv_b, w_down):
    nseq, t, d = x_prompt.shape
    nsamp = x_sample.shape[0]
    depth = w_in.shape[0]
    xp = x_prompt.reshape(nseq * t, d)
    xs = x_sample.reshape(nsamp, d)
    pad = (-(nseq + nsamp)) % SUBLANES
    c_all = jnp.concatenate([c_prompt, c_sample, jnp.zeros((pad, d), F32)], axis=0)
    rel_tiles = _rel_bias_tiles(rel_bias, ATTN_TILE)

    pool_shape = cache_fox_k.shape[:3]
    caches = tuple(c.reshape(pool_shape[0], pool_shape[1], pool_shape[2] * N_KV, HEAD_DIM)
                   for c in (cache_fox_k, cache_fox_v, cache_sb_k, cache_sb_v, cache_dsa_k, cache_dsa_v))
    lf_cache8 = jnp.pad(jnp.swapaxes(cache_fox_logf, 2, 3), ((0, 0), (0, 0), (0, SUBLANES - N_HEADS), (0, 0)))
    past = page_table.shape[1] * pool_shape[2]
    rel_dec = past - jnp.arange(N_KV * past, dtype=jnp.int32) // N_KV
    bias_dec = jnp.pad(_bucket_bias(rel_bias, _t5_bucket(rel_dec)), ((0, SUBLANES - N_HEADS), (0, 0)))
    bias0 = jnp.broadcast_to(jnp.pad(rel_bias[0].astype(F32), (0, SUBLANES - N_HEADS))[:, None], (SUBLANES, LANES))

    st_prompt, st_sample, params, mods = [], [], [], []
    for l in range(depth):
        p = _layer_params(l, w_mod, b_mod, g_mix_pre, g_mix_post, g_ffn_pre, g_ffn_post, w_in, b_fox_f, gdn_a_log,
                          gdn_dt_bias, gdn_conv_w, gdn_norm_w, w_out, w_up, ffn_conv_w, ffn_conv_b, w_down)
        mod = _modulation(c_all, p['w_mod'], p['b_mod'])
        xp, sp = _prompt_layer(xp, mod[:nseq], p, rel_tiles, nseq, t)
        st_prompt.append(sp)
        params.append(p)
        mods.append(mod)
    xp, xs = lax.optimization_barrier((xp, xs))
    for l in range(depth):
        xs, ss = _sample_layer(l, xs, mods[l][nseq:nseq + nsamp], params[l], caches, lf_cache8, cache_dsa_kidx,
                               page_table, bias_dec, bias0, state_gdn_s[l], state_gdn_conv[l], state_ffn_conv[l])
        st_sample.append(ss)
    sp = [jnp.stack(z) for z in zip(*st_prompt)]
    ss = [jnp.stack(z) for z in zip(*st_sample)]
    out = [xp.reshape(nseq, t, d), xs.reshape(nsamp, 1, d)]
    for a, b in zip(sp, ss):
        out += [a, b]
    return tuple(out)
```
